```python
import math
import jax, jax.numpy as jnp
from jax import lax
import numpy as np

D_MODEL = 1024
BATCH = 16
SEQ = 2048
DEPTH = 1

MLA_HEADS = 8
MLA_NOPE = 64
MLA_ROPE = 32
MLA_V = 64
Q_LORA = 384
KV_LORA = 256
ROPE_THETA = 10000.0
MLA_QK = MLA_NOPE + MLA_ROPE
MLA_WIDTH = MLA_HEADS * MLA_V

DIL_HEADS = 8
DIL_HEAD_DIM = 64
DIL_PATTERNS = ((128, 1), (512, 4), (2048, 16))
DIL_WIDTH = DIL_HEADS * DIL_HEAD_DIM

BLOCK = 128
N_BRANCH = 2
D_FF = 4 * D_MODEL
LN_EPS = 1e-5
RMS_EPS = 1e-6
NEG = -1e30
ALPHA = (2 * DEPTH) ** 0.25
BETA = (8 * DEPTH) ** -0.25

SPLITS = (Q_LORA,
          Q_LORA + KV_LORA,
          Q_LORA + KV_LORA + MLA_ROPE,
          Q_LORA + KV_LORA + MLA_ROPE + 3 * DIL_WIDTH)
IN_WIDTH = Q_LORA + KV_LORA + MLA_ROPE + 3 * DIL_WIDTH + N_BRANCH * D_MODEL

kernel_name = 'hybrid_mla_dilated_gated_deepnorm'


def layer_norm(x, g, b):
    xf = x.astype(jnp.float32)
    mu = jnp.mean(xf, axis=-1, keepdims=True)
    var = jnp.mean(jnp.square(xf - mu), axis=-1, keepdims=True)
    y = (xf - mu) * lax.rsqrt(var + LN_EPS) * g.astype(jnp.float32) + b.astype(jnp.float32)
    return y.astype(x.dtype)


def rms_norm(x, g):
    xf = x.astype(jnp.float32)
    y = xf * lax.rsqrt(jnp.mean(jnp.square(xf), axis=-1, keepdims=True) + RMS_EPS)
    return (y * g.astype(jnp.float32)).astype(x.dtype)


def apply_rope(t, pos):
    half = t.shape[-1] // 2
    inv = jnp.power(ROPE_THETA, -jnp.arange(half, dtype=jnp.float32) / half)
    ang = pos.astype(jnp.float32)[:, None] * inv[None, :]
    cos = jnp.cos(ang)[None, :, None, :]
    sin = jnp.sin(ang)[None, :, None, :]
    tf = t.astype(jnp.float32)
    t1, t2 = tf[..., :half], tf[..., half:]
    return jnp.concatenate([t1 * cos - t2 * sin, t1 * sin + t2 * cos], axis=-1).astype(t.dtype)


def alibi_slopes(n):
    return jnp.asarray([2.0 ** (-8.0 * (i + 1) / n) for i in range(n)], dtype=jnp.float32)


def mla_attention(q_a, kv_a, k_r, g_q_a, w_uq, g_kv_a, w_ukv):
    B, S, _ = q_a.shape
    pos = jnp.arange(S)
    q = (rms_norm(q_a, g_q_a) @ w_uq).reshape(B, S, MLA_HEADS, MLA_QK)
    q = jnp.concatenate([q[..., :MLA_NOPE], apply_rope(q[..., MLA_NOPE:], pos)], axis=-1)
    kv = (rms_norm(kv_a, g_kv_a) @ w_ukv).reshape(B, S, MLA_HEADS, MLA_NOPE + MLA_V)
    k_rope = apply_rope(k_r[:, :, None, :], pos)
    k = jnp.concatenate([kv[..., :MLA_NOPE],
                         jnp.broadcast_to(k_rope, (B, S, MLA_HEADS, MLA_ROPE))], axis=-1)
    v = kv[..., MLA_NOPE:]
    scale = MLA_QK ** -0.5
    nb = S // BLOCK
    q_blocks = q.reshape(B, nb, BLOCK, MLA_HEADS, MLA_QK).transpose(1, 0, 2, 3, 4)
    kpos = jnp.arange(S)

    def one_block(args):
        qb, i = args
        s = jnp.einsum('bqhd,bkhd->bhqk', qb, k).astype(jnp.float32) * scale
        qpos = i * BLOCK + jnp.arange(BLOCK)
        s = jnp.where((kpos[None, :] <= qpos[:, None])[None, None], s, NEG)
        p = jax.nn.softmax(s, axis=-1).astype(v.dtype)
        return jnp.einsum('bhqk,bkhd->bqhd', p, v)

    o = lax.map(one_block, (q_blocks, jnp.arange(nb)))
    return o.transpose(1, 0, 2, 3, 4).reshape(B, S, MLA_WIDTH)


def dilated_pattern(q, k, v, window, dilation, slopes):
    B, S, H, Dh = q.shape
    L = S // dilation
    J = window // dilation
    C = min(BLOCK, L)
    nb = -(-L // C)
    Lp = nb * C

    def to_classes(t):
        t = t.reshape(B, L, dilation, H, Dh).transpose(0, 2, 1, 3, 4)
        t = jnp.pad(t, ((0, 0), (0, 0), (0, Lp - L), (0, 0), (0, 0)))
        return t.reshape(B, dilation, nb, C, H, Dh)

    def with_prev(t):
        prev = jnp.pad(t, ((0, 0), (0, 0), (1, 0), (0, 0), (0, 0), (0, 0)))[:, :, :-1]
        return jnp.concatenate([prev, t], axis=3)

    def from_classes(t):
        X = t.shape[-1]
        t = t.reshape(B, dilation, Lp, H, X)[:, :, :L]
        return t.transpose(0, 2, 1, 3, 4).reshape(B, S, H, X)

    qc = to_classes(q)
    kc = with_prev(to_classes(k))
    vc = with_prev(to_classes(v))
    s = jnp.einsum('brnqhd,brnkhd->brnhqk', qc, kc).astype(jnp.float32) * (Dh ** -0.5)
    qi = jnp.arange(C)[:, None] + C
    ki = jnp.arange(2 * C)[None, :]
    dist = qi - ki
    valid = (dist >= 0) & (dist <= J)
    key_exists = (jnp.arange(nb)[:, None] > 0) | (jnp.arange(2 * C)[None, :] >= C)
    mask = valid[None] & key_exists[:, None, :]
    bias = -slopes[:, None, None] * (dilation * dist).astype(jnp.float32)[None]
    s = jnp.where(mask[None, None, :, None], s + bias, NEG)
    m = jnp.max(s, axis=-1)
    p = jnp.exp(s - m[..., None])
    den = jnp.sum(p, axis=-1)
    o = jnp.einsum('brnhqk,brnkhd->brnqhd', p, vc.astype(jnp.float32))
    den_t = den.transpose(0, 1, 2, 4, 3)[..., None]
    m_t = m.transpose(0, 1, 2, 4, 3)[..., None]
    return from_classes(o / den_t), from_classes(m_t), from_classes(den_t)


def dilated_attention(q, k, v):
    slopes = alibi_slopes(DIL_HEADS)
    res = [dilated_pattern(q, k, v, w, d, slopes) for (w, d) in DIL_PATTERNS]
    m_all = res[0][1]
    for r in res[1:]:
        m_all = jnp.maximum(m_all, r[1])
    num = 0.0
    tot = 0.0
    for o, m, den in res:
        wgt = den * jnp.exp(m - m_all)
        num = num + wgt * o
        tot = tot + wgt
    return (num / tot).astype(q.dtype)


def hybrid_layer(x, w_in, b_gate, g_q_a, w_uq, g_kv_a, w_ukv, w_o_mla, w_o_dil,
                 w_out, ln1_g, ln1_b, w_ff1, w_ff2, ln2_g, ln2_b):
    B, S, D = x.shape
    proj = x @ w_in
    q_a, kv_a, k_r, qkv_d, gates = jnp.split(proj, SPLITS, axis=-1)
    y_a = mla_attention(q_a, kv_a, k_r, g_q_a, w_uq, g_kv_a, w_ukv) @ w_o_mla
    qkv_d = qkv_d.reshape(B, S, 3, DIL_HEADS, DIL_HEAD_DIM)
    o_b = dilated_attention(qkv_d[:, :, 0], qkv_d[:, :, 1], qkv_d[:, :, 2])
    y_b = o_b.reshape(B, S, DIL_WIDTH) @ w_o_dil
    g = jax.nn.sigmoid(gates.reshape(B, S, N_BRANCH, D) + b_gate)
    mixed = (g[:, :, 0] * y_a + g[:, :, 1] * y_b) @ w_out
    h = layer_norm(ALPHA * x + mixed, ln1_g, ln1_b)
    f = jnp.square(jax.nn.relu(h @ w_ff1)) @ w_ff2
    return layer_norm(ALPHA * h + f, ln2_g, ln2_b)


def setup_inputs(seed: int = 0) -> dict:
    key = jax.random.key(seed)
    ks = jax.random.split(key, 20)
    f32 = jnp.float32

    def nrm(k, shape, fan_in, scale=1.0):
        return jax.random.normal(k, shape, f32) * (fan_in ** -0.5) * scale

    def gain(k, shape):
        return 1.0 + 0.02 * jax.random.normal(k, shape, f32)

    def small(k, shape):
        return 0.02 * jax.random.normal(k, shape, f32)

    L_ = DEPTH
    return {
        'x': jax.random.normal(ks[0], (BATCH, SEQ, D_MODEL), f32),
        'w_in': nrm(ks[1], (L_, D_MODEL, IN_WIDTH), D_MODEL),
        'b_gate': small(ks[2], (L_, N_BRANCH, D_MODEL)),
        'g_q_a': gain(ks[3], (L_, Q_LORA)),
        'w_uq': nrm(ks[4], (L_, Q_LORA, MLA_HEADS * MLA_QK), Q_LORA),
        'g_kv_a': gain(ks[5], (L_, KV_LORA)),
        'w_ukv': nrm(ks[6], (L_, KV_LORA, MLA_HEADS * (MLA_NOPE + MLA_V)), KV_LORA),
        'w_o_mla': nrm(ks[7], (L_, MLA_WIDTH, D_MODEL), MLA_WIDTH, BETA),
        'w_o_dil': nrm(ks[8], (L_, DIL_WIDTH, D_MODEL), DIL_WIDTH, BETA),
        'w_out': nrm(ks[9], (L_, D_MODEL, D_MODEL), D_MODEL, BETA),
        'ln1_g': gain(ks[10], (L_, D_MODEL)),
        'ln1_b': small(ks[11], (L_, D_MODEL)),
        'w_ff1': nrm(ks[12], (L_, D_MODEL, D_FF), D_MODEL, BETA),
        'w_ff2': nrm(ks[13], (L_, D_FF, D_MODEL), D_FF, BETA),
        'ln2_g': gain(ks[14], (L_, D_MODEL)),
        'ln2_b': small(ks[15], (L_, D_MODEL)),
    }


def reference(x, w_in, b_gate, g_q_a, w_uq, g_kv_a, w_ukv, w_o_mla, w_o_dil,
              w_out, ln1_g, ln1_b, w_ff1, w_ff2, ln2_g, ln2_b):
    for l in range(DEPTH):
        x = hybrid_layer(x, w_in[l], b_gate[l], g_q_a[l], w_uq[l], g_kv_a[l], w_ukv[l],
                         w_o_mla[l], w_o_dil[l], w_out[l], ln1_g[l], ln1_b[l],
                         w_ff1[l], w_ff2[l], ln2_g[l], ln2_b[l])
    return x
```

```python
import functools
import math

import jax
import jax.numpy as jnp
import numpy as np
from jax import lax
from jax.experimental import pallas as pl
from jax.experimental.pallas import tpu as pltpu

D_MODEL = 1024
SEQ = 2048

MLA_HEADS = 8
MLA_NOPE = 64
MLA_ROPE = 32
MLA_V = 64
Q_LORA = 384
KV_LORA = 256
ROPE_THETA = 10000.0
MLA_QK = MLA_NOPE + MLA_ROPE
MLA_WIDTH = MLA_HEADS * MLA_V
ROPE_HALF = MLA_ROPE // 2

DIL_HEADS = 8
DIL_HEAD_DIM = 64
DIL_PATTERNS = ((128, 1), (512, 4), (2048, 16))
DIL_WIDTH = DIL_HEADS * DIL_HEAD_DIM

N_BRANCH = 2
D_FF = 4 * D_MODEL
LN_EPS = 1e-5
RMS_EPS = 1e-6
NEG = -1e30

LOG2E = math.log2(math.e)

LANES = 128
MXU_DIM = 256
VMEM_LIMIT = 56 * 1024 * 1024

HEAD_PAD = LANES
ROW_TILE = 512
ATT_TILE = MXU_DIM
N_ATT_TILES = SEQ // ATT_TILE
FF_CHUNK = 1024

C_QA = 0
C_KVA = C_QA + Q_LORA
C_QD = C_KVA + KV_LORA
C_KD = C_QD + DIL_WIDTH
C_VD = C_KD + DIL_WIDTH
C_GATE = C_VD + DIL_WIDTH
C_KR = C_GATE + N_BRANCH * D_MODEL
IN_AUG = C_KR + LANES

BF16 = jnp.bfloat16
F32 = jnp.float32


def _dot(a, b):
    return jnp.dot(a, b, preferred_element_type=F32)


def _rms(t, g):
    return t * lax.rsqrt(jnp.mean(t * t, axis=-1, keepdims=True) + RMS_EPS) * g


def _layer_norm(t, g, b):
    mu = jnp.mean(t, axis=-1, keepdims=True)
    d = t - mu
    var = jnp.mean(d * d, axis=-1, keepdims=True)
    return d * lax.rsqrt(var + LN_EPS) * g + b


def _proj_kernel(x_ref, win_ref, wqT_ref, wk_ref, wvT_ref, gq_ref, gkv_ref, bg_ref,
                 cq_ref, sq_ref, ck_ref, sk_ref,
                 qT_ref, kn_ref, kr_ref, vT_ref, qdT_ref, kd_ref, vdT_ref, g_ref):
    xb = x_ref[...].astype(BF16)

    lat = _dot(xb, win_ref[:, C_QA:C_QD])
    qn = _rms(lat[:, :Q_LORA], gq_ref[...])
    kvn = _rms(lat[:, Q_LORA:], gkv_ref[...])
    qnT = qn.T.astype(BF16)
    kvnT = kvn.T.astype(BF16)

    qT = _dot(wqT_ref[...], qnT)
    cq = cq_ref[...]
    sq = sq_ref[...]
    qscale = (MLA_QK ** -0.5) * LOG2E
    for h in range(MLA_HEADS):
        b = h * HEAD_PAD
        qT_ref[b:b + MLA_NOPE, :] = (qT[b:b + MLA_NOPE, :] * qscale).astype(BF16)
        t1 = qT[b + MLA_NOPE:b + MLA_NOPE + ROPE_HALF, :]
        t2 = qT[b + MLA_NOPE + ROPE_HALF:b + MLA_QK, :]
        qT_ref[b + MLA_NOPE:b + MLA_NOPE + ROPE_HALF, :] = (t1 * cq - t2 * sq).astype(BF16)
        qT_ref[b + MLA_NOPE + ROPE_HALF:b + MLA_QK, :] = (t1 * sq + t2 * cq).astype(BF16)
        qT_ref[b + MLA_QK:b + HEAD_PAD, :] = jnp.zeros((HEAD_PAD - MLA_QK, qT.shape[1]), BF16)

    kn_ref[...] = _dot(kvn.astype(BF16), wk_ref[...]).astype(BF16)
    vT_ref[...] = _dot(wvT_ref[...], kvnT).astype(BF16)

    dil = _dot(xb, win_ref[:, C_QD:C_GATE])
    dscale = (DIL_HEAD_DIM ** -0.5) * LOG2E
    qdT_ref[...] = (dil[:, :DIL_WIDTH] * dscale).T.astype(BF16)
    kd_ref[...] = dil[:, DIL_WIDTH:2 * DIL_WIDTH].astype(BF16)
    vdT_ref[...] = dil[:, 2 * DIL_WIDTH:].T.astype(BF16)

    gates = _dot(xb, win_ref[:, C_GATE:C_KR]) + bg_ref[...]
    g_ref[...] = jax.nn.sigmoid(gates).astype(BF16)

    kr = _dot(xb, win_ref[:, C_KR:IN_AUG])
    kr = kr * ck_ref[...] + pltpu.roll(kr * sk_ref[...], LANES - MLA_ROPE, axis=1)
    kr_ref[...] = kr.astype(BF16)


def _proj_call(x2, win, wqT, wk, wvT, gq, gkv, bg, cq, sq, ck, sk):
    T = x2.shape[0]
    tm = ROW_TILE
    tiles_per_seq = SEQ // tm
    const = lambda i: (0, 0)
    row = lambda i: (i, 0)
    col = lambda i: (0, i)
    pos_col = lambda i: (0, i % tiles_per_seq)
    pos_row = lambda i: (i % tiles_per_seq, 0)

    def wspec(a):
        return pl.BlockSpec(a.shape, const, pipeline_mode=pl.Buffered(1))

    out_shapes = (
        jax.ShapeDtypeStruct((MLA_HEADS * HEAD_PAD, T), BF16),
        jax.ShapeDtypeStruct((T, MLA_HEADS * HEAD_PAD), BF16),
        jax.ShapeDtypeStruct((T, LANES), BF16),
        jax.ShapeDtypeStruct((MLA_WIDTH, T), BF16),
        jax.ShapeDtypeStruct((DIL_WIDTH, T), BF16),
        jax.ShapeDtypeStruct((T, DIL_WIDTH), BF16),
        jax.ShapeDtypeStruct((DIL_WIDTH, T), BF16),
        jax.ShapeDtypeStruct((T, N_BRANCH * D_MODEL), BF16),
    )
    out_specs = (
        pl.BlockSpec((MLA_HEADS * HEAD_PAD, tm), col),
        pl.BlockSpec((tm, MLA_HEADS * HEAD_PAD), row),
        pl.BlockSpec((tm, LANES), row),
        pl.BlockSpec((MLA_WIDTH, tm), col),
        pl.BlockSpec((DIL_WIDTH, tm), col),
        pl.BlockSpec((tm, DIL_WIDTH), row),
        pl.BlockSpec((DIL_WIDTH, tm), col),
        pl.BlockSpec((tm, N_BRANCH * D_MODEL), row),
    )
    in_specs = [
        pl.BlockSpec((tm, D_MODEL), row),
        wspec(win), wspec(wqT), wspec(wk), wspec(wvT), wspec(gq), wspec(gkv), wspec(bg),
        pl.BlockSpec((ROPE_HALF, tm), pos_col),
        pl.BlockSpec((ROPE_HALF, tm), pos_col),
        pl.BlockSpec((tm, LANES), pos_row),
        pl.BlockSpec((tm, LANES), pos_row),
    ]
    return pl.pallas_call(
        _proj_kernel,
        grid=(T // tm,),
        in_specs=in_specs,
        out_specs=out_specs,
        out_shape=out_shapes,
        compiler_params=pltpu.CompilerParams(
            dimension_semantics=("arbitrary",), vmem_limit_bytes=VMEM_LIMIT),
        name="proj",
    )(x2, win, wqT, wk, wvT, gq, gkv, bg, cq, sq, ck, sk)


def _attend(q_tile, i, key_chunk, val_chunk, bias_chunk, diag_bias, d_v):
    tq = q_tile.shape[1]

    def update(carry, sT, vc):
        m, l, acc = carry
        m_new = jnp.maximum(m, jnp.max(sT, axis=0, keepdims=True))
        alpha = jnp.exp2(m - m_new)
        p = jnp.exp2(sT - m_new)
        l = alpha * l + jnp.sum(p, axis=0, keepdims=True)
        acc = alpha * acc + _dot(vc, p.astype(BF16))
        return m_new, l, acc

    def step(j, carry):
        sT = _dot(key_chunk(j), q_tile)
        b = bias_chunk(j)
        if b is not None:
            sT = sT + b
        return update(carry, sT, val_chunk(j))

    carry = (jnp.full((1, tq), NEG, F32), jnp.zeros((1, tq), F32), jnp.zeros((d_v, tq), F32))
    if i > 0:
        carry = lax.fori_loop(0, i, step, carry)
    sT = _dot(key_chunk(i), q_tile) + diag_bias
    _, l, acc = update(carry, sT, val_chunk(i))
    return acc / l


def _causal_bias():
    key = lax.broadcasted_iota(jnp.int32, (ATT_TILE, ATT_TILE), 0)
    qry = lax.broadcasted_iota(jnp.int32, (ATT_TILE, ATT_TILE), 1)
    return jnp.where(key > qry, NEG, 0.0).astype(F32)


def _mla_kernel(qT_ref, kn_ref, kr_ref, vT_ref, oT_ref, kf_ref, vc_ref):
    kf_ref[...] = kn_ref[...] + kr_ref[...]
    for j in range(N_ATT_TILES):
        vc_ref[j] = vT_ref[:, j * ATT_TILE:(j + 1) * ATT_TILE]
    diag = _causal_bias()

    def key_chunk(j):
        return kf_ref[pl.ds(pl.multiple_of(j * ATT_TILE, ATT_TILE), ATT_TILE), :]

    for i in range(N_ATT_TILES):
        q_tile = qT_ref[:, i * ATT_TILE:(i + 1) * ATT_TILE]
        o = _attend(q_tile, i, key_chunk, lambda j: vc_ref[j], lambda j: None, diag, MLA_V)
        oT_ref[:, i * ATT_TILE:(i + 1) * ATT_TILE] = o.astype(BF16)


def _mla_call(qT, kn, kr, vT, batch):
    T = kn.shape[0]
    return pl.pallas_call(
        _mla_kernel,
        grid=(batch, MLA_HEADS),
        in_specs=[
            pl.BlockSpec((HEAD_PAD, SEQ), lambda b, h: (h, b)),
            pl.BlockSpec((SEQ, HEAD_PAD), lambda b, h: (b, h)),
            pl.BlockSpec((SEQ, LANES), lambda b, h: (b, 0)),
            pl.BlockSpec((MLA_V, SEQ), lambda b, h: (h, b)),
        ],
        out_specs=pl.BlockSpec((MLA_V, SEQ), lambda b, h: (h, b)),
        out_shape=jax.ShapeDtypeStruct((MLA_WIDTH, T), BF16),
        scratch_shapes=[
            pltpu.VMEM((SEQ, HEAD_PAD), BF16),
            pltpu.VMEM((N_ATT_TILES, MLA_V, ATT_TILE), BF16),
        ],
        compiler_params=pltpu.CompilerParams(
            dimension_semantics=("arbitrary", "arbitrary"), vmem_limit_bytes=VMEM_LIMIT),
        name="mla_attn",
    )(qT, kn, kr, vT)


def _dil_kernel(qT_ref, k_ref, vT_ref, bias_ref, oT_ref, q2_ref, vc_ref):
    odd = pl.program_id(0) % 2
    q = qT_ref[...]
    zero = jnp.zeros_like(q)
    q2_ref[0:DIL_HEAD_DIM, :] = jnp.where(odd == 0, q, zero)
    q2_ref[DIL_HEAD_DIM:2 * DIL_HEAD_DIM, :] = jnp.where(odd == 1, q, zero)
    for j in range(N_ATT_TILES):
        vc_ref[j] = vT_ref[:, j * ATT_TILE:(j + 1) * ATT_TILE]

    def key_chunk(j):
        return k_ref[pl.ds(pl.multiple_of(j * ATT_TILE, ATT_TILE), ATT_TILE), :]

    for i in range(N_ATT_TILES):
        q_tile = q2_ref[:, i * ATT_TILE:(i + 1) * ATT_TILE]
        o = _attend(q_tile, i, key_chunk, lambda j: vc_ref[j],
                    lambda j: bias_ref[i - j], bias_ref[0], DIL_HEAD_DIM)
        oT_ref[:, i * ATT_TILE:(i + 1) * ATT_TILE] = o.astype(BF16)


def _dil_call(qdT, kd, vdT, bias, batch):
    T = kd.shape[0]
    return pl.pallas_call(
        _dil_kernel,
        grid=(DIL_HEADS, batch),
        in_specs=[
            pl.BlockSpec((DIL_HEAD_DIM, SEQ), lambda h, b: (h, b)),
            pl.BlockSpec((SEQ, 2 * DIL_HEAD_DIM), lambda h, b: (b, h // 2)),
            pl.BlockSpec((DIL_HEAD_DIM, SEQ), lambda h, b: (h, b)),
            pl.BlockSpec((None, N_ATT_TILES, ATT_TILE, ATT_TILE), lambda h, b: (h, 0, 0, 0)),
        ],
        out_specs=pl.BlockSpec((DIL_HEAD_DIM, SEQ), lambda h, b: (h, b)),
        out_shape=jax.ShapeDtypeStruct((DIL_WIDTH, T), BF16),
        scratch_shapes=[
            pltpu.VMEM((2 * DIL_HEAD_DIM, SEQ), BF16),
            pltpu.VMEM((N_ATT_TILES, DIL_HEAD_DIM, ATT_TILE), BF16),
        ],
        compiler_params=pltpu.CompilerParams(
            dimension_semantics=("arbitrary", "arbitrary"), vmem_limit_bytes=VMEM_LIMIT),
        name="dil_attn",
    )(qdT, kd, vdT, bias)


def _dilated_bias():
    off = np.arange(N_ATT_TILES)[:, None, None]
    key = np.arange(ATT_TILE)[None, :, None]
    qry = np.arange(ATT_TILE)[None, None, :]
    dist = ATT_TILE * off + qry - key
    count = np.zeros(dist.shape, np.int32)
    for window, dilation in DIL_PATTERNS:
        count += ((dist >= 0) & (dist % dilation == 0) & (dist <= window)).astype(np.int32)
    slopes = jnp.asarray([2.0 ** (-8.0 * (i + 1) / DIL_HEADS) for i in range(DIL_HEADS)], F32)
    logc = jnp.asarray(np.log2(np.maximum(count, 1)), F32)
    alibi = -(slopes * LOG2E)[:, None, None, None] * jnp.asarray(dist, F32)[None]
    return jnp.where(jnp.asarray(count > 0)[None], alibi + logc[None], NEG)


def _post_kernel(alpha, x_ref, oaT_ref, obT_ref, g_ref, woa_ref, wob_ref, wout_ref,
                 l1g_ref, l1b_ref, w1_ref, w2_ref, l2g_ref, l2b_ref, out_ref):
    ya = _dot(oaT_ref[...].T, woa_ref[...])
    yb = _dot(obT_ref[...].T, wob_ref[...])
    g = g_ref[...].astype(F32)
    mix = (g[:, :D_MODEL] * ya + g[:, D_MODEL:] * yb).astype(BF16)
    mixed = _dot(mix, wout_ref[...])
    h = _layer_norm(alpha * x_ref[...] + mixed, l1g_ref[...], l1b_ref[...])
    hb = h.astype(BF16)
    f = jnp.zeros_like(h)
    for c in range(D_FF // FF_CHUNK):
        u = jnp.maximum(_dot(hb, w1_ref[:, c * FF_CHUNK:(c + 1) * FF_CHUNK]), 0.0)
        f = f + _dot((u * u).astype(BF16), w2_ref[c * FF_CHUNK:(c + 1) * FF_CHUNK, :])
    out_ref[...] = _layer_norm(alpha * h + f, l2g_ref[...], l2b_ref[...])


def _post_call(alpha, x2, oaT, obT, g, woa, wob, wout, l1g, l1b, w1, w2, l2g, l2b):
    T = x2.shape[0]
    tm = ROW_TILE
    const = lambda i: (0, 0)
    row = lambda i: (i, 0)
    col = lambda i: (0, i)

    def wspec(a):
        return pl.BlockSpec(a.shape, const, pipeline_mode=pl.Buffered(1))

    return pl.pallas_call(
        functools.partial(_post_kernel, alpha),
        grid=(T // tm,),
        in_specs=[
            pl.BlockSpec((tm, D_MODEL), row),
            pl.BlockSpec((MLA_WIDTH, tm), col),
            pl.BlockSpec((DIL_WIDTH, tm), col),
            pl.BlockSpec((tm, N_BRANCH * D_MODEL), row),
            wspec(woa), wspec(wob), wspec(wout), wspec(l1g), wspec(l1b),
            wspec(w1), wspec(w2), wspec(l2g), wspec(l2b),
        ],
        out_specs=pl.BlockSpec((tm, D_MODEL), row),
        out_shape=jax.ShapeDtypeStruct((T, D_MODEL), F32),
        compiler_params=pltpu.CompilerParams(
            dimension_semantics=("arbitrary",), vmem_limit_bytes=VMEM_LIMIT),
        name="post",
    )(x2, oaT, obT, g, woa, wob, wout, l1g, l1b, w1, w2, l2g, l2b)


def _rotate_half_cols(w):
    half = w.shape[-1] // 2
    return jnp.concatenate([-w[..., half:], w[..., :half]], axis=-1)


def _rope_tables():
    inv = jnp.power(ROPE_THETA, -jnp.arange(ROPE_HALF, dtype=F32) / ROPE_HALF)
    ang = jnp.arange(SEQ).astype(F32)[:, None] * inv[None, :]
    cos, sin = jnp.cos(ang), jnp.sin(ang)
    qscale = (MLA_QK ** -0.5) * LOG2E
    cq = (cos * qscale).T
    sq = (sin * qscale).T
    zeros = jnp.zeros((SEQ, MLA_NOPE), F32)
    z32 = jnp.zeros((SEQ, MLA_ROPE), F32)
    ck = jnp.concatenate([zeros, cos, cos, z32], axis=1)
    sk = jnp.concatenate([zeros, z32, sin, sin], axis=1)
    return cq, sq, ck, sk


def _layer(x2, batch, alpha, w_in, b_gate, g_q_a, w_uq, g_kv_a, w_ukv, w_o_mla, w_o_dil,
           w_out, ln1_g, ln1_b, w_ff1, w_ff2, ln2_g, ln2_b):
    s0, s1, s2, s3 = Q_LORA, Q_LORA + KV_LORA, Q_LORA + KV_LORA + MLA_ROPE, \
        Q_LORA + KV_LORA + MLA_ROPE + 3 * DIL_WIDTH
    w_kr = w_in[:, s1:s2]
    win = jnp.concatenate(
        [w_in[:, :s1], w_in[:, s2:s3], w_in[:, s3:],
         jnp.zeros((D_MODEL, MLA_NOPE), F32), w_kr, _rotate_half_cols(w_kr)], axis=1).astype(BF16)

    wq = w_uq.reshape(Q_LORA, MLA_HEADS, MLA_QK)
    wq = jnp.pad(wq, ((0, 0), (0, 0), (0, HEAD_PAD - MLA_QK)))
    wqT = wq.reshape(Q_LORA, MLA_HEADS * HEAD_PAD).T.astype(BF16)
    wkv = w_ukv.reshape(KV_LORA, MLA_HEADS, MLA_NOPE + MLA_V)
    wk = jnp.pad(wkv[:, :, :MLA_NOPE], ((0, 0), (0, 0), (0, HEAD_PAD - MLA_NOPE)))
    wk = wk.reshape(KV_LORA, MLA_HEADS * HEAD_PAD).astype(BF16)
    wvT = wkv[:, :, MLA_NOPE:].reshape(KV_LORA, MLA_WIDTH).T.astype(BF16)

    cq, sq, ck, sk = _rope_tables()
    qT, kn, kr, vT, qdT, kd, vdT, g = _proj_call(
        x2, win, wqT, wk, wvT, g_q_a[None], g_kv_a[None],
        b_gate.reshape(1, N_BRANCH * D_MODEL), cq, sq, ck, sk)

    oaT = _mla_call(qT, kn, kr, vT, batch)
    obT = _dil_call(qdT, kd, vdT, _dilated_bias(), batch)

    return _post_call(alpha, x2, oaT, obT, g, w_o_mla.astype(BF16), w_o_dil.astype(BF16),
                      w_out.astype(BF16), ln1_g[None], ln1_b[None], w_ff1.astype(BF16),
                      w_ff2.astype(BF16), ln2_g[None], ln2_b[None])


def kernel(x, w_in, b_gate, g_q_a, w_uq, g_kv_a, w_ukv, w_o_mla, w_o_dil, w_out,
           ln1_g, ln1_b, w_ff1, w_ff2, ln2_g, ln2_b):
    batch, seq, d = x.shape
    assert (seq, d) == (SEQ, D_MODEL)
    depth = w_in.shape[0]
    alpha = (2 * depth) ** 0.25
    x2 = x.reshape(batch * seq, d)
    for l in range(depth):
        x2 = _layer(x2, batch, alpha, w_in[l], b_gate[l], g_q_a[l], w_uq[l], g_kv_a[l],
                    w_ukv[l], w_o_mla[l], w_o_dil[l], w_out[l], ln1_g[l], ln1_b[l],
                    w_ff1[l], w_ff2[l], ln2_g[l], ln2_b[l])
    return x2.reshape(batch, seq, d)
```

```python
import functools
import math

import jax
import jax.numpy as jnp
import numpy as np
from jax import lax
from jax.experimental import pallas as pl
from jax.experimental.pallas import tpu as pltpu

D_MODEL = 1024
SEQ = 2048

MLA_HEADS = 8
MLA_NOPE = 64
MLA_ROPE = 32
MLA_V = 64
Q_LORA = 384
KV_LORA = 256
ROPE_THETA = 10000.0
MLA_QK = MLA_NOPE + MLA_ROPE
MLA_WIDTH = MLA_HEADS * MLA_V
ROPE_HALF = MLA_ROPE // 2

DIL_HEADS = 8
DIL_HEAD_DIM = 64
DIL_PATTERNS = ((128, 1), (512, 4), (2048, 16))
DIL_WIDTH = DIL_HEADS * DIL_HEAD_DIM

N_BRANCH = 2
D_FF = 4 * D_MODEL
LN_EPS = 1e-5
RMS_EPS = 1e-6
NEG = -1e30

LOG2E = math.log2(math.e)

LANES = 128
MXU_DIM = 256
VMEM_LIMIT = 56 * 1024 * 1024

HEAD_PAD = LANES
ROW_TILE = 512
ATT_TILE = MXU_DIM
N_ATT_TILES = SEQ // ATT_TILE
FF_CHUNK = 1024

C_QA = 0
C_KVA = C_QA + Q_LORA
C_QD = C_KVA + KV_LORA
C_KD = C_QD + DIL_WIDTH
C_VD = C_KD + DIL_WIDTH
C_GATE = C_VD + DIL_WIDTH
C_KR = C_GATE + N_BRANCH * D_MODEL
IN_AUG = C_KR + LANES

BF16 = jnp.bfloat16
F32 = jnp.float32


def _dot(a, b):
    return jnp.dot(a, b, preferred_element_type=F32)


def _rms(t, g):
    return t * lax.rsqrt(jnp.mean(t * t, axis=-1, keepdims=True) + RMS_EPS) * g


def _layer_norm(t, g, b):
    mu = jnp.mean(t, axis=-1, keepdims=True)
    d = t - mu
    var = jnp.mean(d * d, axis=-1, keepdims=True)
    return d * lax.rsqrt(var + LN_EPS) * g + b


def _proj_kernel(x_ref, win_ref, wqT_ref, wk_ref, wvT_ref, gq_ref, gkv_ref, bg_ref,
                 cq_ref, sq_ref, ck_ref, sk_ref,
                 qT_ref, kn_ref, kr_ref, vT_ref, qdT_ref, kd_ref, vdT_ref, g_ref):
    xb = x_ref[...].astype(BF16)

    lat = _dot(xb, win_ref[:, C_QA:C_QD])
    qn = _rms(lat[:, :Q_LORA], gq_ref[...])
    kvn = _rms(lat[:, Q_LORA:], gkv_ref[...])
    qnT = qn.T.astype(BF16)
    kvnT = kvn.T.astype(BF16)

    qT = _dot(wqT_ref[...], qnT)
    cq = cq_ref[...]
    sq = sq_ref[...]
    qscale = (MLA_QK ** -0.5) * LOG2E
    for h in range(MLA_HEADS):
        b = h * HEAD_PAD
        qT_ref[b:b + MLA_NOPE, :] = (qT[b:b + MLA_NOPE, :] * qscale).astype(BF16)
        t1 = qT[b + MLA_NOPE:b + MLA_NOPE + ROPE_HALF, :]
        t2 = qT[b + MLA_NOPE + ROPE_HALF:b + MLA_QK, :]
        qT_ref[b + MLA_NOPE:b + MLA_NOPE + ROPE_HALF, :] = (t1 * cq - t2 * sq).astype(BF16)
        qT_ref[b + MLA_NOPE + ROPE_HALF:b + MLA_QK, :] = (t1 * sq + t2 * cq).astype(BF16)
        qT_ref[b + MLA_QK:b + HEAD_PAD, :] = jnp.zeros((HEAD_PAD - MLA_QK, qT.shape[1]), BF16)

    kn_ref[...] = _dot(kvn.astype(BF16), wk_ref[...]).astype(BF16)
    vT_ref[...] = _dot(wvT_ref[...], kvnT).astype(BF16)

    dil = _dot(xb, win_ref[:, C_QD:C_GATE])
    dscale = (DIL_HEAD_DIM ** -0.5) * LOG2E
    qdT_ref[...] = (dil[:, :DIL_WIDTH] * dscale).T.astype(BF16)
    kd_ref[...] = dil[:, DIL_WIDTH:2 * DIL_WIDTH].astype(BF16)
    vdT_ref[...] = dil[:, 2 * DIL_WIDTH:].T.astype(BF16)

    gates = _dot(xb, win_ref[:, C_GATE:C_KR]) + bg_ref[...]
    g_ref[...] = jax.nn.sigmoid(gates).astype(BF16)

    kr = _dot(xb, win_ref[:, C_KR:IN_AUG])
    kr = kr * ck_ref[...] + pltpu.roll(kr * sk_ref[...], LANES - MLA_ROPE, axis=1)
    kr_ref[...] = kr.astype(BF16)


def _proj_call(x2, win, wqT, wk, wvT, gq, gkv, bg, cq, sq, ck, sk):
    T = x2.shape[0]
    tm = ROW_TILE
    tiles_per_seq = SEQ // tm
    const = lambda i: (0, 0)
    row = lambda i: (i, 0)
    col = lambda i: (0, i)
    pos_col = lambda i: (0, i % tiles_per_seq)
    pos_row = lambda i: (i % tiles_per_seq, 0)

    def wspec(a):
        return pl.BlockSpec(a.shape, const, pipeline_mode=pl.Buffered(1))

    out_shapes = (
        jax.ShapeDtypeStruct((MLA_HEADS * HEAD_PAD, T), BF16),
        jax.ShapeDtypeStruct((T, MLA_HEADS * HEAD_PAD), BF16),
        jax.ShapeDtypeStruct((T, LANES), BF16),
        jax.ShapeDtypeStruct((MLA_WIDTH, T), BF16),
        jax.ShapeDtypeStruct((DIL_WIDTH, T), BF16),
        jax.ShapeDtypeStruct((T, DIL_WIDTH), BF16),
        jax.ShapeDtypeStruct((DIL_WIDTH, T), BF16),
        jax.ShapeDtypeStruct((T, N_BRANCH * D_MODEL), BF16),
    )
    out_specs = (
        pl.BlockSpec((MLA_HEADS * HEAD_PAD, tm), col),
        pl.BlockSpec((tm, MLA_HEADS * HEAD_PAD), row),
        pl.BlockSpec((tm, LANES), row),
        pl.BlockSpec((MLA_WIDTH, tm), col),
        pl.BlockSpec((DIL_WIDTH, tm), col),
        pl.BlockSpec((tm, DIL_WIDTH), row),
        pl.BlockSpec((DIL_WIDTH, tm), col),
        pl.BlockSpec((tm, N_BRANCH * D_MODEL), row),
    )
    in_specs = [
        pl.BlockSpec((tm, D_MODEL), row),
        wspec(win), wspec(wqT), wspec(wk), wspec(wvT), wspec(gq), wspec(gkv), wspec(bg),
        pl.BlockSpec((ROPE_HALF, tm), pos_col),
        pl.BlockSpec((ROPE_HALF, tm), pos_col),
        pl.BlockSpec((tm, LANES), pos_row),
        pl.BlockSpec((tm, LANES), pos_row),
    ]
    return pl.pallas_call(
        _proj_kernel,
        grid=(T // tm,),
        in_specs=in_specs,
        out_specs=out_specs,
        out_shape=out_shapes,
        compiler_params=pltpu.CompilerParams(
            dimension_semantics=("arbitrary",), vmem_limit_bytes=VMEM_LIMIT),
        name="proj",
    )(x2, win, wqT, wk, wvT, gq, gkv, bg, cq, sq, ck, sk)


def _attend(q_tile, i, keys_ref, valsT_ref, bias_chunk, s_ref, p_ref):
    n = (i + 1) * ATT_TILE
    m = None
    for j in range(i + 1):
        rows = slice(j * ATT_TILE, (j + 1) * ATT_TILE)
        sT = _dot(keys_ref[rows, :], q_tile)
        b = bias_chunk(j)
        if b is not None:
            sT = sT + b
        s_ref[rows, :] = sT
        cm = jnp.max(sT, axis=0, keepdims=True)
        m = cm if m is None else jnp.maximum(m, cm)
    l = None
    for j in range(i + 1):
        rows = slice(j * ATT_TILE, (j + 1) * ATT_TILE)
        p = jnp.exp2(s_ref[rows, :] - m)
        cl = jnp.sum(p, axis=0, keepdims=True)
        l = cl if l is None else l + cl
        p_ref[rows, :] = p.astype(BF16)
    return _dot(valsT_ref[:, 0:n], p_ref[0:n, :]) / l


def _causal_bias():
    key = lax.broadcasted_iota(jnp.int32, (ATT_TILE, ATT_TILE), 0)
    qry = lax.broadcasted_iota(jnp.int32, (ATT_TILE, ATT_TILE), 1)
    return jnp.where(key > qry, NEG, 0.0).astype(F32)


_ATT_SCRATCH = [
    pltpu.VMEM((2, SEQ, ATT_TILE), F32),
    pltpu.VMEM((2, SEQ, ATT_TILE), BF16),
]


def _mla_kernel(qT_ref, kn_ref, kr_ref, vT_ref, oT_ref, kf_ref, s_ref, p_ref):
    kf_ref[...] = kn_ref[...] + kr_ref[...]
    diag = _causal_bias()
    for i in range(N_ATT_TILES):
        cols = slice(i * ATT_TILE, (i + 1) * ATT_TILE)
        o = _attend(qT_ref[:, cols], i, kf_ref, vT_ref,
                    lambda j: diag if j == i else None, s_ref.at[i % 2], p_ref.at[i % 2])
        oT_ref[:, cols] = o.astype(BF16)


def _mla_call(qT, kn, kr, vT, batch):
    T = kn.shape[0]
    return pl.pallas_call(
        _mla_kernel,
        grid=(batch, MLA_HEADS),
        in_specs=[
            pl.BlockSpec((HEAD_PAD, SEQ), lambda b, h: (h, b)),
            pl.BlockSpec((SEQ, HEAD_PAD), lambda b, h: (b, h)),
            pl.BlockSpec((SEQ, LANES), lambda b, h: (b, 0)),
            pl.BlockSpec((MLA_V, SEQ), lambda b, h: (h, b)),
        ],
        out_specs=pl.BlockSpec((MLA_V, SEQ), lambda b, h: (h, b)),
        out_shape=jax.ShapeDtypeStruct((MLA_WIDTH, T), BF16),
        scratch_shapes=[pltpu.VMEM((SEQ, HEAD_PAD), BF16)] + _ATT_SCRATCH,
        compiler_params=pltpu.CompilerParams(
            dimension_semantics=("arbitrary", "arbitrary"), vmem_limit_bytes=VMEM_LIMIT),
        name="mla_attn",
    )(qT, kn, kr, vT)


def _dil_kernel(qT_ref, k_ref, vT_ref, bias_ref, oT_ref, q2_ref, s_ref, p_ref):
    odd = pl.program_id(0) % 2
    q = qT_ref[...]
    zero = jnp.zeros_like(q)
    q2_ref[0:DIL_HEAD_DIM, :] = jnp.where(odd == 0, q, zero)
    q2_ref[DIL_HEAD_DIM:2 * DIL_HEAD_DIM, :] = jnp.where(odd == 1, q, zero)
    for i in range(N_ATT_TILES):
        cols = slice(i * ATT_TILE, (i + 1) * ATT_TILE)
        o = _attend(q2_ref[:, cols], i, k_ref, vT_ref,
                    lambda j: bias_ref[i - j], s_ref.at[i % 2], p_ref.at[i % 2])
        oT_ref[:, cols] = o.astype(BF16)


def _dil_call(qdT, kd, vdT, bias, batch):
    T = kd.shape[0]
    return pl.pallas_call(
        _dil_kernel,
        grid=(DIL_HEADS, batch),
        in_specs=[
            pl.BlockSpec((DIL_HEAD_DIM, SEQ), lambda h, b: (h, b)),
            pl.BlockSpec((SEQ, 2 * DIL_HEAD_DIM), lambda h, b: (b, h // 2)),
            pl.BlockSpec((DIL_HEAD_DIM, SEQ), lambda h, b: (h, b)),
            pl.BlockSpec((None, N_ATT_TILES, ATT_TILE, ATT_TILE), lambda h, b: (h, 0, 0, 0)),
        ],
        out_specs=pl.BlockSpec((DIL_HEAD_DIM, SEQ), lambda h, b: (h, b)),
        out_shape=jax.ShapeDtypeStruct((DIL_WIDTH, T), BF16),
        scratch_shapes=[pltpu.VMEM((2 * DIL_HEAD_DIM, SEQ), BF16)] + _ATT_SCRATCH,
        compiler_params=pltpu.CompilerParams(
            dimension_semantics=("arbitrary", "arbitrary"), vmem_limit_bytes=VMEM_LIMIT),
        name="dil_attn",
    )(qdT, kd, vdT, bias)


def _dilated_bias():
    off = np.arange(N_ATT_TILES)[:, None, None]
    key = np.arange(ATT_TILE)[None, :, None]
    qry = np.arange(ATT_TILE)[None, None, :]
    dist = ATT_TILE * off + qry - key
    count = np.zeros(dist.shape, np.int32)
    for window, dilation in DIL_PATTERNS:
        count += ((dist >= 0) & (dist % dilation == 0) & (dist <= window)).astype(np.int32)
    slopes = jnp.asarray([2.0 ** (-8.0 * (i + 1) / DIL_HEADS) for i in range(DIL_HEADS)], F32)
    logc = jnp.asarray(np.log2(np.maximum(count, 1)), F32)
    alibi = -(slopes * LOG2E)[:, None, None, None] * jnp.asarray(dist, F32)[None]
    return jnp.where(jnp.asarray(count > 0)[None], alibi + logc[None], NEG)


def _post_kernel(alpha, x_ref, oaT_ref, obT_ref, g_ref, woa_ref, wob_ref, wout_ref,
                 l1g_ref, l1b_ref, w1_ref, w2_ref, l2g_ref, l2b_ref, out_ref):
    ya = _dot(oaT_ref[...].T, woa_ref[...])
    yb = _dot(obT_ref[...].T, wob_ref[...])
    g = g_ref[...].astype(F32)
    mix = (g[:, :D_MODEL] * ya + g[:, D_MODEL:] * yb).astype(BF16)
    mixed = _dot(mix, wout_ref[...])
    h = _layer_norm(alpha * x_ref[...] + mixed, l1g_ref[...], l1b_ref[...])
    hb = h.astype(BF16)
    f = jnp.zeros_like(h)
    for c in range(D_FF // FF_CHUNK):
        u = jnp.maximum(_dot(hb, w1_ref[:, c * FF_CHUNK:(c + 1) * FF_CHUNK]), 0.0)
        f = f + _dot((u * u).astype(BF16), w2_ref[c * FF_CHUNK:(c + 1) * FF_CHUNK, :])
    out_ref[...] = _layer_norm(alpha * h + f, l2g_ref[...], l2b_ref[...])


def _post_call(alpha, x2, oaT, obT, g, woa, wob, wout, l1g, l1b, w1, w2, l2g, l2b):
    T = x2.shape[0]
    tm = ROW_TILE
    const = lambda i: (0, 0)
    row = lambda i: (i, 0)
    col = lambda i: (0, i)

    def wspec(a):
        return pl.BlockSpec(a.shape, const, pipeline_mode=pl.Buffered(1))

    return pl.pallas_call(
        functools.partial(_post_kernel, alpha),
        grid=(T // tm,),
        in_specs=[
            pl.BlockSpec((tm, D_MODEL), row),
            pl.BlockSpec((MLA_WIDTH, tm), col),
            pl.BlockSpec((DIL_WIDTH, tm), col),
            pl.BlockSpec((tm, N_BRANCH * D_MODEL), row),
            wspec(woa), wspec(wob), wspec(wout), wspec(l1g), wspec(l1b),
            wspec(w1), wspec(w2), wspec(l2g), wspec(l2b),
        ],
        out_specs=pl.BlockSpec((tm, D_MODEL), row),
        out_shape=jax.ShapeDtypeStruct((T, D_MODEL), F32),
        compiler_params=pltpu.CompilerParams(
            dimension_semantics=("arbitrary",), vmem_limit_bytes=VMEM_LIMIT),
        name="post",
    )(x2, oaT, obT, g, woa, wob, wout, l1g, l1b, w1, w2, l2g, l2b)


def _rotate_half_cols(w):
    half = w.shape[-1] // 2
    return jnp.concatenate([-w[..., half:], w[..., :half]], axis=-1)


def _rope_tables():
    inv = jnp.power(ROPE_THETA, -jnp.arange(ROPE_HALF, dtype=F32) / ROPE_HALF)
    ang = jnp.arange(SEQ).astype(F32)[:, None] * inv[None, :]
    cos, sin = jnp.cos(ang), jnp.sin(ang)
    qscale = (MLA_QK ** -0.5) * LOG2E
    cq = (cos * qscale).T
    sq = (sin * qscale).T
    zeros = jnp.zeros((SEQ, MLA_NOPE), F32)
    z32 = jnp.zeros((SEQ, MLA_ROPE), F32)
    ck = jnp.concatenate([zeros, cos, cos, z32], axis=1)
    sk = jnp.concatenate([zeros, z32, sin, sin], axis=1)
    return cq, sq, ck, sk


def _layer(x2, batch, alpha, w_in, b_gate, g_q_a, w_uq, g_kv_a, w_ukv, w_o_mla, w_o_dil,
           w_out, ln1_g, ln1_b, w_ff1, w_ff2, ln2_g, ln2_b):
    s0, s1, s2, s3 = Q_LORA, Q_LORA + KV_LORA, Q_LORA + KV_LORA + MLA_ROPE, \
        Q_LORA + KV_LORA + MLA_ROPE + 3 * DIL_WIDTH
    w_kr = w_in[:, s1:s2]
    win = jnp.concatenate(
        [w_in[:, :s1], w_in[:, s2:s3], w_in[:, s3:],
         jnp.zeros((D_MODEL, MLA_NOPE), F32), w_kr, _rotate_half_cols(w_kr)], axis=1).astype(BF16)

    wq = w_uq.reshape(Q_LORA, MLA_HEADS, MLA_QK)
    wq = jnp.pad(wq, ((0, 0), (0, 0), (0, HEAD_PAD - MLA_QK)))
    wqT = wq.reshape(Q_LORA, MLA_HEADS * HEAD_PAD).T.astype(BF16)
    wkv = w_ukv.reshape(KV_LORA, MLA_HEADS, MLA_NOPE + MLA_V)
    wk = jnp.pad(wkv[:, :, :MLA_NOPE], ((0, 0), (0, 0), (0, HEAD_PAD - MLA_NOPE)))
    wk = wk.reshape(KV_LORA, MLA_HEADS * HEAD_PAD).astype(BF16)
    wvT = wkv[:, :, MLA_NOPE:].reshape(KV_LORA, MLA_WIDTH).T.astype(BF16)

    cq, sq, ck, sk = _rope_tables()
    qT, kn, kr, vT, qdT, kd, vdT, g = _proj_call(
        x2, win, wqT, wk, wvT, g_q_a[None], g_kv_a[None],
        b_gate.reshape(1, N_BRANCH * D_MODEL), cq, sq, ck, sk)

    oaT = _mla_call(qT, kn, kr, vT, batch)
    obT = _dil_call(qdT, kd, vdT, _dilated_bias(), batch)

    return _post_call(alpha, x2, oaT, obT, g, w_o_mla.astype(BF16), w_o_dil.astype(BF16),
                      w_out.astype(BF16), ln1_g[None], ln1_b[None], w_ff1.astype(BF16),
                      w_ff2.astype(BF16), ln2_g[None], ln2_b[None])


def kernel(x, w_in, b_gate, g_q_a, w_uq, g_kv_a, w_ukv, w_o_mla, w_o_dil, w_out,
           ln1_g, ln1_b, w_ff1, w_ff2, ln2_g, ln2_b):
    batch, seq, d = x.shape
    assert (seq, d) == (SEQ, D_MODEL)
    depth = w_in.shape[0]
    alpha = (2 * depth) ** 0.25
    x2 = x.reshape(batch * seq, d)
    for l in range(depth):
        x2 = _layer(x2, batch, alpha, w_in[l], b_gate[l], g_q_a[l], w_uq[l], g_kv_a[l],
                    w_ukv[l], w_o_mla[l], w_o_dil[l], w_out[l], ln1_g[l], ln1_b[l],
                    w_ff1[l], w_ff2[l], ln2_g[l], ln2_b[l])
    return x2.reshape(batch, seq, d)
```

```python
import functools
import math

import jax
import jax.numpy as jnp
import numpy as np
from jax import lax
from jax.experimental import pallas as pl
from jax.experimental.pallas import tpu as pltpu

D_MODEL = 1024
SEQ = 2048

MLA_HEADS = 8
MLA_NOPE = 64
MLA_ROPE = 32
MLA_V = 64
Q_LORA = 384
KV_LORA = 256
ROPE_THETA = 10000.0
MLA_QK = MLA_NOPE + MLA_ROPE
MLA_WIDTH = MLA_HEADS * MLA_V
ROPE_HALF = MLA_ROPE // 2

DIL_HEADS = 8
DIL_HEAD_DIM = 64
DIL_PATTERNS = ((128, 1), (512, 4), (2048, 16))
DIL_WIDTH = DIL_HEADS * DIL_HEAD_DIM

N_BRANCH = 2
D_FF = 4 * D_MODEL
LN_EPS = 1e-5
RMS_EPS = 1e-6
NEG = -1e30

LOG2E = math.log2(math.e)

LANES = 128
MXU_DIM = 256
VMEM_LIMIT = 56 * 1024 * 1024

HEAD_PAD = LANES
ROW_TILE = 512
ATT_Q = 2 * MXU_DIM
ATT_K = MXU_DIM
N_Q_TILES = SEQ // ATT_Q
SCORE_LEAD = 2
FF_CHUNK = 1024

C_QA = 0
C_KVA = C_QA + Q_LORA
C_QD = C_KVA + KV_LORA
C_KD = C_QD + DIL_WIDTH
C_VD = C_KD + DIL_WIDTH
C_GATE = C_VD + DIL_WIDTH
C_KR = C_GATE + N_BRANCH * D_MODEL
IN_AUG = C_KR + LANES

BF16 = jnp.bfloat16
F32 = jnp.float32


def _dot(a, b):
    return jnp.dot(a, b, preferred_element_type=F32)


def _rms(t, g):
    return t * lax.rsqrt(jnp.mean(t * t, axis=-1, keepdims=True) + RMS_EPS) * g


def _layer_norm(t, g, b):
    mu = jnp.mean(t, axis=-1, keepdims=True)
    d = t - mu
    var = jnp.mean(d * d, axis=-1, keepdims=True)
    return d * lax.rsqrt(var + LN_EPS) * g + b


def _proj_kernel(x_ref, win_ref, wqT_ref, wk_ref, wvT_ref, gq_ref, gkv_ref, bg_ref,
                 cq_ref, sq_ref, ck_ref, sk_ref,
                 qT_ref, kn_ref, kr_ref, vT_ref, qdT_ref, kd_ref, vdT_ref, g_ref):
    xb = x_ref[...].astype(BF16)

    lat = _dot(xb, win_ref[:, C_QA:C_QD])
    qn = _rms(lat[:, :Q_LORA], gq_ref[...])
    kvn = _rms(lat[:, Q_LORA:], gkv_ref[...])
    qnT = qn.T.astype(BF16)
    kvnT = kvn.T.astype(BF16)

    qT = _dot(wqT_ref[...], qnT)
    cq = cq_ref[...]
    sq = sq_ref[...]
    qscale = (MLA_QK ** -0.5) * LOG2E
    for h in range(MLA_HEADS):
        b = h * HEAD_PAD
        qT_ref[b:b + MLA_NOPE, :] = (qT[b:b + MLA_NOPE, :] * qscale).astype(BF16)
        t1 = qT[b + MLA_NOPE:b + MLA_NOPE + ROPE_HALF, :]
        t2 = qT[b + MLA_NOPE + ROPE_HALF:b + MLA_QK, :]
        qT_ref[b + MLA_NOPE:b + MLA_NOPE + ROPE_HALF, :] = (t1 * cq - t2 * sq).astype(BF16)
        qT_ref[b + MLA_NOPE + ROPE_HALF:b + MLA_QK, :] = (t1 * sq + t2 * cq).astype(BF16)
        qT_ref[b + MLA_QK:b + HEAD_PAD, :] = jnp.zeros((HEAD_PAD - MLA_QK, qT.shape[1]), BF16)

    kn_ref[...] = _dot(kvn.astype(BF16), wk_ref[...]).astype(BF16)
    vT_ref[...] = _dot(wvT_ref[...], kvnT).astype(BF16)

    dil = _dot(xb, win_ref[:, C_QD:C_GATE])
    dscale = (DIL_HEAD_DIM ** -0.5) * LOG2E
    qdT_ref[...] = (dil[:, :DIL_WIDTH] * dscale).T.astype(BF16)
    kd_ref[...] = dil[:, DIL_WIDTH:2 * DIL_WIDTH].astype(BF16)
    vdT_ref[...] = dil[:, 2 * DIL_WIDTH:].T.astype(BF16)

    gates = _dot(xb, win_ref[:, C_GATE:C_KR]) + bg_ref[...]
    g_ref[...] = jax.nn.sigmoid(gates).astype(BF16)

    kr = _dot(xb, win_ref[:, C_KR:IN_AUG])
    kr = kr * ck_ref[...] + pltpu.roll(kr * sk_ref[...], LANES - MLA_ROPE, axis=1)
    kr_ref[...] = kr.astype(BF16)


def _proj_call(x2, win, wqT, wk, wvT, gq, gkv, bg, cq, sq, ck, sk):
    T = x2.shape[0]
    tm = ROW_TILE
    tiles_per_seq = SEQ // tm
    const = lambda i: (0, 0)
    row = lambda i: (i, 0)
    col = lambda i: (0, i)
    pos_col = lambda i: (0, i % tiles_per_seq)
    pos_row = lambda i: (i % tiles_per_seq, 0)

    def wspec(a):
        return pl.BlockSpec(a.shape, const, pipeline_mode=pl.Buffered(1))

    out_shapes = (
        jax.ShapeDtypeStruct((MLA_HEADS * HEAD_PAD, T), BF16),
        jax.ShapeDtypeStruct((T, MLA_HEADS * HEAD_PAD), BF16),
        jax.ShapeDtypeStruct((T, LANES), BF16),
        jax.ShapeDtypeStruct((MLA_WIDTH, T), BF16),
        jax.ShapeDtypeStruct((DIL_WIDTH, T), BF16),
        jax.ShapeDtypeStruct((T, DIL_WIDTH), BF16),
        jax.ShapeDtypeStruct((DIL_WIDTH, T), BF16),
        jax.ShapeDtypeStruct((T, N_BRANCH * D_MODEL), BF16),
    )
    out_specs = (
        pl.BlockSpec((MLA_HEADS * HEAD_PAD, tm), col),
        pl.BlockSpec((tm, MLA_HEADS * HEAD_PAD), row),
        pl.BlockSpec((tm, LANES), row),
        pl.BlockSpec((MLA_WIDTH, tm), col),
        pl.BlockSpec((DIL_WIDTH, tm), col),
        pl.BlockSpec((tm, DIL_WIDTH), row),
        pl.BlockSpec((DIL_WIDTH, tm), col),
        pl.BlockSpec((tm, N_BRANCH * D_MODEL), row),
    )
    in_specs = [
        pl.BlockSpec((tm, D_MODEL), row),
        wspec(win), wspec(wqT), wspec(wk), wspec(wvT), wspec(gq), wspec(gkv), wspec(bg),
        pl.BlockSpec((ROPE_HALF, tm), pos_col),
        pl.BlockSpec((ROPE_HALF, tm), pos_col),
        pl.BlockSpec((tm, LANES), pos_row),
        pl.BlockSpec((tm, LANES), pos_row),
    ]
    return pl.pallas_call(
        _proj_kernel,
        grid=(T // tm,),
        in_specs=in_specs,
        out_specs=out_specs,
        out_shape=out_shapes,
        compiler_params=pltpu.CompilerParams(
            dimension_semantics=("arbitrary",), vmem_limit_bytes=VMEM_LIMIT),
        name="proj",
    )(x2, win, wqT, wk, wvT, gq, gkv, bg, cq, sq, ck, sk)


def _causal_attention(qT_ref, keys_ref, valsT_ref, bias_chunk, s_refs, oT_ref):
    def qcols(i):
        return slice(i * ATT_Q, (i + 1) * ATT_Q)

    def krows(j):
        return slice(j * ATT_K, (j + 1) * ATT_K)

    def first_col(i, j):
        return max(0, j * ATT_K - i * ATT_Q)

    def widen(x, lo, fill):
        if lo == 0:
            return x
        return jnp.concatenate([jnp.full((x.shape[0], lo), fill, x.dtype), x], axis=1)

    def scores(i, j, q_tile):
        lo = first_col(i, j)
        sT = _dot(keys_ref[krows(j), :], q_tile[:, lo:])
        b = bias_chunk(i, j)
        if b is not None:
            sT = sT + b[:, lo:]
        s_refs[i][krows(j), lo:] = sT
        return widen(jnp.max(sT, axis=0, keepdims=True), lo, NEG)

    def weights(i, j, m):
        lo = first_col(i, j)
        p = jnp.exp2(s_refs[i][krows(j), lo:] - m[:, lo:])
        return widen(_dot(valsT_ref[:, krows(j)], p.astype(BF16)), lo, 0.0)

    def fold(op, old, new):
        return new if old is None else op(old, new)

    def all_scores(i):
        q_tile, m = qT_ref[:, qcols(i)], None
        for j in range(_n_chunks(i)):
            m = fold(jnp.maximum, m, scores(i, j, q_tile))
        return m

    order = list(range(N_Q_TILES))[::-1]
    m = {i: all_scores(i) for i in order[:SCORE_LEAD]}
    for pos, i in enumerate(order):
        ahead = order[pos + SCORE_LEAD] if pos + SCORE_LEAD < len(order) else None
        q_ahead = qT_ref[:, qcols(ahead)] if ahead is not None else None
        n_ahead = _n_chunks(ahead) if ahead is not None else 0
        m_ahead, acc = None, None
        for j in range(max(_n_chunks(i), n_ahead)):
            if j < _n_chunks(i):
                acc = fold(jnp.add, acc, weights(i, j, m[i]))
            if j < n_ahead:
                m_ahead = fold(jnp.maximum, m_ahead, scores(ahead, j, q_ahead))
        if ahead is not None:
            m[ahead] = m_ahead
        d_v = oT_ref.shape[0]
        oT_ref[:, qcols(i)] = (acc[:d_v, :] / acc[d_v:d_v + 1, :]).astype(BF16)


def _n_chunks(i):
    return (i + 1) * (ATT_Q // ATT_K)


def _causal_bias(chunk_offset):
    key = lax.broadcasted_iota(jnp.int32, (ATT_K, ATT_Q), 0) + chunk_offset
    qry = lax.broadcasted_iota(jnp.int32, (ATT_K, ATT_Q), 1)
    return jnp.where(key > qry, NEG, 0.0).astype(F32)


ONES_ROWS = 16


def _values_with_ones(vT_ref, va_ref):
    d_v = vT_ref.shape[0]
    va_ref[0:d_v, :] = vT_ref[...]
    va_ref[d_v:, :] = jnp.ones((ONES_ROWS, va_ref.shape[1]), BF16)


def _att_scratch(d_v):
    return [pltpu.VMEM((d_v + ONES_ROWS, SEQ), BF16)] + [
        pltpu.VMEM((_n_chunks(i) * ATT_K, ATT_Q), F32) for i in range(N_Q_TILES)]


def _mla_kernel(qT_ref, kn_ref, kr_ref, vT_ref, oT_ref, kf_ref, va_ref, *s_refs):
    kf_ref[...] = kn_ref[...] + kr_ref[...]
    _values_with_ones(vT_ref, va_ref)
    per_tile = ATT_Q // ATT_K
    masks = [_causal_bias(c * ATT_K) for c in range(per_tile)]

    def bias_chunk(i, j):
        c = j - i * per_tile
        return masks[c] if c >= 0 else None

    _causal_attention(qT_ref, kf_ref, va_ref, bias_chunk, s_refs, oT_ref)


def _mla_call(qT, kn, kr, vT, batch):
    T = kn.shape[0]
    return pl.pallas_call(
        _mla_kernel,
        grid=(batch, MLA_HEADS),
        in_specs=[
            pl.BlockSpec((HEAD_PAD, SEQ), lambda b, h: (h, b)),
            pl.BlockSpec((SEQ, HEAD_PAD), lambda b, h: (b, h)),
            pl.BlockSpec((SEQ, LANES), lambda b, h: (b, 0)),
            pl.BlockSpec((MLA_V, SEQ), lambda b, h: (h, b)),
        ],
        out_specs=pl.BlockSpec((MLA_V, SEQ), lambda b, h: (h, b)),
        out_shape=jax.ShapeDtypeStruct((MLA_WIDTH, T), BF16),
        scratch_shapes=[pltpu.VMEM((SEQ, HEAD_PAD), BF16)] + _att_scratch(MLA_V),
        compiler_params=pltpu.CompilerParams(
            dimension_semantics=("arbitrary", "arbitrary"), vmem_limit_bytes=VMEM_LIMIT),
        name="mla_attn",
    )(qT, kn, kr, vT)


N_DIL_OFFSETS = SEQ // ATT_K + 1


def _dil_kernel(qT_ref, k_ref, vT_ref, bias_ref, oT_ref, q2_ref, va_ref, *s_refs):
    _values_with_ones(vT_ref, va_ref)
    odd = pl.program_id(0) % 2
    q = qT_ref[...]
    zero = jnp.zeros_like(q)
    q2_ref[0:DIL_HEAD_DIM, :] = jnp.where(odd == 0, q, zero)
    q2_ref[DIL_HEAD_DIM:2 * DIL_HEAD_DIM, :] = jnp.where(odd == 1, q, zero)
    per_tile = ATT_Q // ATT_K

    def bias_chunk(i, j):
        return jnp.concatenate(
            [bias_ref[i * per_tile + c - j + 1] for c in range(per_tile)], axis=1)

    _causal_attention(q2_ref, k_ref, va_ref, bias_chunk, s_refs, oT_ref)


def _dil_call(qdT, kd, vdT, bias, batch):
    T = kd.shape[0]
    return pl.pallas_call(
        _dil_kernel,
        grid=(DIL_HEADS, batch),
        in_specs=[
            pl.BlockSpec((DIL_HEAD_DIM, SEQ), lambda h, b: (h, b)),
            pl.BlockSpec((SEQ, 2 * DIL_HEAD_DIM), lambda h, b: (b, h // 2)),
            pl.BlockSpec((DIL_HEAD_DIM, SEQ), lambda h, b: (h, b)),
            pl.BlockSpec((None, N_DIL_OFFSETS, ATT_K, ATT_K), lambda h, b: (h, 0, 0, 0)),
        ],
        out_specs=pl.BlockSpec((DIL_HEAD_DIM, SEQ), lambda h, b: (h, b)),
        out_shape=jax.ShapeDtypeStruct((DIL_WIDTH, T), BF16),
        scratch_shapes=[pltpu.VMEM((2 * DIL_HEAD_DIM, SEQ), BF16)] + _att_scratch(DIL_HEAD_DIM),
        compiler_params=pltpu.CompilerParams(
            dimension_semantics=("arbitrary", "arbitrary"), vmem_limit_bytes=VMEM_LIMIT),
        name="dil_attn",
    )(qdT, kd, vdT, bias)


def _dilated_bias():
    off = np.arange(-1, N_DIL_OFFSETS - 1)[:, None, None]
    key = np.arange(ATT_K)[None, :, None]
    qry = np.arange(ATT_K)[None, None, :]
    dist = ATT_K * off + qry - key
    count = np.zeros(dist.shape, np.int32)
    for window, dilation in DIL_PATTERNS:
        count += ((dist >= 0) & (dist % dilation == 0) & (dist <= window)).astype(np.int32)
    slopes = jnp.asarray([2.0 ** (-8.0 * (i + 1) / DIL_HEADS) for i in range(DIL_HEADS)], F32)
    logc = jnp.asarray(np.log2(np.maximum(count, 1)), F32)
    alibi = -(slopes * LOG2E)[:, None, None, None] * jnp.asarray(dist, F32)[None]
    return jnp.where(jnp.asarray(count > 0)[None], alibi + logc[None], NEG)


def _post_kernel(alpha, x_ref, oaT_ref, obT_ref, g_ref, woa_ref, wob_ref, wout_ref,
                 l1g_ref, l1b_ref, w1_ref, w2_ref, l2g_ref, l2b_ref, out_ref):
    ya = _dot(oaT_ref[...].T, woa_ref[...])
    yb = _dot(obT_ref[...].T, wob_ref[...])
    g = g_ref[...].astype(F32)
    mix = (g[:, :D_MODEL] * ya + g[:, D_MODEL:] * yb).astype(BF16)
    mixed = _dot(mix, wout_ref[...])
    h = _layer_norm(alpha * x_ref[...] + mixed, l1g_ref[...], l1b_ref[...])
    hb = h.astype(BF16)
    f = jnp.zeros_like(h)
    for c in range(D_FF // FF_CHUNK):
        u = jnp.maximum(_dot(hb, w1_ref[:, c * FF_CHUNK:(c + 1) * FF_CHUNK]), 0.0)
        f = f + _dot((u * u).astype(BF16), w2_ref[c * FF_CHUNK:(c + 1) * FF_CHUNK, :])
    out_ref[...] = _layer_norm(alpha * h + f, l2g_ref[...], l2b_ref[...])


def _post_call(alpha, x2, oaT, obT, g, woa, wob, wout, l1g, l1b, w1, w2, l2g, l2b):
    T = x2.shape[0]
    tm = ROW_TILE
    const = lambda i: (0, 0)
    row = lambda i: (i, 0)
    col = lambda i: (0, i)

    def wspec(a):
        return pl.BlockSpec(a.shape, const, pipeline_mode=pl.Buffered(1))

    return pl.pallas_call(
        functools.partial(_post_kernel, alpha),
        grid=(T // tm,),
        in_specs=[
            pl.BlockSpec((tm, D_MODEL), row),
            pl.BlockSpec((MLA_WIDTH, tm), col),
            pl.BlockSpec((DIL_WIDTH, tm), col),
            pl.BlockSpec((tm, N_BRANCH * D_MODEL), row),
            wspec(woa), wspec(wob), wspec(wout), wspec(l1g), wspec(l1b),
            wspec(w1), wspec(w2), wspec(l2g), wspec(l2b),
        ],
        out_specs=pl.BlockSpec((tm, D_MODEL), row),
        out_shape=jax.ShapeDtypeStruct((T, D_MODEL), F32),
        compiler_params=pltpu.CompilerParams(
            dimension_semantics=("arbitrary",), vmem_limit_bytes=VMEM_LIMIT),
        name="post",
    )(x2, oaT, obT, g, woa, wob, wout, l1g, l1b, w1, w2, l2g, l2b)


def _rotate_half_cols(w):
    half = w.shape[-1] // 2
    return jnp.concatenate([-w[..., half:], w[..., :half]], axis=-1)


def _rope_tables():
    inv = jnp.power(ROPE_THETA, -jnp.arange(ROPE_HALF, dtype=F32) / ROPE_HALF)
    ang = jnp.arange(SEQ).astype(F32)[:, None] * inv[None, :]
    cos, sin = jnp.cos(ang), jnp.sin(ang)
    qscale = (MLA_QK ** -0.5) * LOG2E
    cq = (cos * qscale).T
    sq = (sin * qscale).T
    zeros = jnp.zeros((SEQ, MLA_NOPE), F32)
    z32 = jnp.zeros((SEQ, MLA_ROPE), F32)
    ck = jnp.concatenate([zeros, cos, cos, z32], axis=1)
    sk = jnp.concatenate([zeros, z32, sin, sin], axis=1)
    return cq, sq, ck, sk


def _layer(x2, batch, alpha, w_in, b_gate, g_q_a, w_uq, g_kv_a, w_ukv, w_o_mla, w_o_dil,
           w_out, ln1_g, ln1_b, w_ff1, w_ff2, ln2_g, ln2_b):
    s0, s1, s2, s3 = Q_LORA, Q_LORA + KV_LORA, Q_LORA + KV_LORA + MLA_ROPE, \
        Q_LORA + KV_LORA + MLA_ROPE + 3 * DIL_WIDTH
    w_kr = w_in[:, s1:s2]
    win = jnp.concatenate(
        [w_in[:, :s1], w_in[:, s2:s3], w_in[:, s3:],
         jnp.zeros((D_MODEL, MLA_NOPE), F32), w_kr, _rotate_half_cols(w_kr)], axis=1).astype(BF16)

    wq = w_uq.reshape(Q_LORA, MLA_HEADS, MLA_QK)
    wq = jnp.pad(wq, ((0, 0), (0, 0), (0, HEAD_PAD - MLA_QK)))
    wqT = wq.reshape(Q_LORA, MLA_HEADS * HEAD_PAD).T.astype(BF16)
    wkv = w_ukv.reshape(KV_LORA, MLA_HEADS, MLA_NOPE + MLA_V)
    wk = jnp.pad(wkv[:, :, :MLA_NOPE], ((0, 0), (0, 0), (0, HEAD_PAD - MLA_NOPE)))
    wk = wk.reshape(KV_LORA, MLA_HEADS * HEAD_PAD).astype(BF16)
    wvT = wkv[:, :, MLA_NOPE:].reshape(KV_LORA, MLA_WIDTH).T.astype(BF16)

    cq, sq, ck, sk = _rope_tables()
    qT, kn, kr, vT, qdT, kd, vdT, g = _proj_call(
        x2, win, wqT, wk, wvT, g_q_a[None], g_kv_a[None],
        b_gate.reshape(1, N_BRANCH * D_MODEL), cq, sq, ck, sk)

    oaT = _mla_call(qT, kn, kr, vT, batch)
    obT = _dil_call(qdT, kd, vdT, _dilated_bias(), batch)

    return _post_call(alpha, x2, oaT, obT, g, w_o_mla.astype(BF16), w_o_dil.astype(BF16),
                      w_out.astype(BF16), ln1_g[None], ln1_b[None], w_ff1.astype(BF16),
                      w_ff2.astype(BF16), ln2_g[None], ln2_b[None])


def kernel(x, w_in, b_gate, g_q_a, w_uq, g_kv_a, w_ukv, w_o_mla, w_o_dil, w_out,
           ln1_g, ln1_b, w_ff1, w_ff2, ln2_g, ln2_b):
    batch, seq, d = x.shape
    assert (seq, d) == (SEQ, D_MODEL)
    depth = w_in.shape[0]
    alpha = (2 * depth) ** 0.25
    x2 = x.reshape(batch * seq, d)
    for l in range(depth):
        x2 = _layer(x2, batch, alpha, w_in[l], b_gate[l], g_q_a[l], w_uq[l], g_kv_a[l],
                    w_ukv[l], w_o_mla[l], w_o_dil[l], w_out[l], ln1_g[l], ln1_b[l],
                    w_ff1[l], w_ff2[l], ln2_g[l], ln2_b[l])
    return x2.reshape(batch, seq, d)
```

```python
import functools
import math
from typing import NamedTuple

import jax
import jax.numpy as jnp
import numpy as np
from jax import lax
from jax.experimental import pallas as pl
from jax.experimental.pallas import tpu as pltpu

D_MODEL = 1024
SEQ = 2048

MLA_HEADS = 8
MLA_NOPE = 64
MLA_ROPE = 32
MLA_V = 64
Q_LORA = 384
KV_LORA = 256
ROPE_THETA = 10000.0
MLA_QK = MLA_NOPE + MLA_ROPE
MLA_WIDTH = MLA_HEADS * MLA_V
ROPE_HALF = MLA_ROPE // 2

DIL_HEADS = 8
DIL_HEAD_DIM = 64
DIL_PATTERNS = ((128, 1), (512, 4), (2048, 16))
DIL_WIDTH = DIL_HEADS * DIL_HEAD_DIM

N_BRANCH = 2
D_FF = 4 * D_MODEL
LN_EPS = 1e-5
RMS_EPS = 1e-6
NEG = -1e30

LOG2E = math.log2(math.e)

LANES = 128
MXU_DIM = 256
VMEM_LIMIT = 56 * 1024 * 1024

HEAD_PAD = LANES
ROW_TILE = 512
ATT_Q = 2 * MXU_DIM
ATT_K = MXU_DIM
N_Q_TILES = SEQ // ATT_Q
HEADS_PER_STEP = 2
SCORE_LEAD = 2
FF_CHUNK = 1024

C_QA = 0
C_KVA = C_QA + Q_LORA
C_KR = C_KVA + KV_LORA
C_QD = C_KR + LANES
C_KD = C_QD + DIL_WIDTH
C_VD = C_KD + DIL_WIDTH
C_GATE = C_VD + DIL_WIDTH
IN_AUG = C_GATE + N_BRANCH * D_MODEL

BF16 = jnp.bfloat16
F32 = jnp.float32


def _dot(a, b):
    return jnp.dot(a, b, preferred_element_type=F32)


def _rms(t, g):
    return t * lax.rsqrt(jnp.mean(t * t, axis=-1, keepdims=True) + RMS_EPS) * g


def _layer_norm(t, g, b):
    mu = jnp.mean(t, axis=-1, keepdims=True)
    d = t - mu
    var = jnp.mean(d * d, axis=-1, keepdims=True)
    return d * lax.rsqrt(var + LN_EPS) * g + b


def _proj_kernel(x_ref, win_ref, wqT_ref, wk_ref, wvT_ref, gq_ref, gkv_ref, bg_ref,
                 cq_ref, sq_ref, ck_ref, sk_ref,
                 qT_ref, kn_ref, kr_ref, vT_ref, qdT_ref, kd_ref, vdT_ref, g_ref):
    xb = x_ref[...].astype(BF16)

    lat = _dot(xb, win_ref[:, C_QA:C_QD])
    qn = _rms(lat[:, C_QA:C_KVA], gq_ref[...])
    kvn = _rms(lat[:, C_KVA:C_KR], gkv_ref[...])
    qnT = qn.T.astype(BF16)
    kvnT = kvn.T.astype(BF16)

    qT = _dot(wqT_ref[...], qnT)
    cq = cq_ref[...]
    sq = sq_ref[...]
    qscale = (MLA_QK ** -0.5) * LOG2E
    for h in range(MLA_HEADS):
        b = h * HEAD_PAD
        qT_ref[b:b + MLA_NOPE, :] = (qT[b:b + MLA_NOPE, :] * qscale).astype(BF16)
        t1 = qT[b + MLA_NOPE:b + MLA_NOPE + ROPE_HALF, :]
        t2 = qT[b + MLA_NOPE + ROPE_HALF:b + MLA_QK, :]
        qT_ref[b + MLA_NOPE:b + MLA_NOPE + ROPE_HALF, :] = (t1 * cq - t2 * sq).astype(BF16)
        qT_ref[b + MLA_NOPE + ROPE_HALF:b + MLA_QK, :] = (t1 * sq + t2 * cq).astype(BF16)
        qT_ref[b + MLA_QK:b + HEAD_PAD, :] = jnp.zeros((HEAD_PAD - MLA_QK, qT.shape[1]), BF16)

    kn_ref[...] = _dot(kvn.astype(BF16), wk_ref[...]).astype(BF16)
    vT_ref[...] = _dot(wvT_ref[...], kvnT).astype(BF16)

    dil = _dot(xb, win_ref[:, C_QD:C_GATE])
    dscale = (DIL_HEAD_DIM ** -0.5) * LOG2E
    qdT_ref[...] = (dil[:, :DIL_WIDTH] * dscale).T.astype(BF16)
    kd_ref[...] = dil[:, DIL_WIDTH:2 * DIL_WIDTH].astype(BF16)
    vdT_ref[...] = dil[:, 2 * DIL_WIDTH:].T.astype(BF16)

    gates = _dot(xb, win_ref[:, C_GATE:IN_AUG]) + bg_ref[...]
    g_ref[...] = jax.nn.sigmoid(gates).astype(BF16)

    kr = lat[:, C_KR:C_QD]
    kr = kr * ck_ref[...] + pltpu.roll(kr * sk_ref[...], LANES - MLA_ROPE, axis=1)
    kr_ref[...] = kr.astype(BF16)


def _proj_call(x2, win, wqT, wk, wvT, gq, gkv, bg, cq, sq, ck, sk):
    T = x2.shape[0]
    tm = ROW_TILE
    tiles_per_seq = SEQ // tm
    const = lambda i: (0, 0)
    row = lambda i: (i, 0)
    col = lambda i: (0, i)
    pos_col = lambda i: (0, i % tiles_per_seq)
    pos_row = lambda i: (i % tiles_per_seq, 0)

    def wspec(a):
        return pl.BlockSpec(a.shape, const, pipeline_mode=pl.Buffered(1))

    out_shapes = (
        jax.ShapeDtypeStruct((MLA_HEADS * HEAD_PAD, T), BF16),
        jax.ShapeDtypeStruct((T, MLA_HEADS * HEAD_PAD), BF16),
        jax.ShapeDtypeStruct((T, LANES), BF16),
        jax.ShapeDtypeStruct((MLA_WIDTH, T), BF16),
        jax.ShapeDtypeStruct((DIL_WIDTH, T), BF16),
        jax.ShapeDtypeStruct((T, DIL_WIDTH), BF16),
        jax.ShapeDtypeStruct((DIL_WIDTH, T), BF16),
        jax.ShapeDtypeStruct((T, N_BRANCH * D_MODEL), BF16),
    )
    out_specs = (
        pl.BlockSpec((MLA_HEADS * HEAD_PAD, tm), col),
        pl.BlockSpec((tm, MLA_HEADS * HEAD_PAD), row),
        pl.BlockSpec((tm, LANES), row),
        pl.BlockSpec((MLA_WIDTH, tm), col),
        pl.BlockSpec((DIL_WIDTH, tm), col),
        pl.BlockSpec((tm, DIL_WIDTH), row),
        pl.BlockSpec((DIL_WIDTH, tm), col),
        pl.BlockSpec((tm, N_BRANCH * D_MODEL), row),
    )
    in_specs = [
        pl.BlockSpec((tm, D_MODEL), row),
        wspec(win), wspec(wqT), wspec(wk), wspec(wvT), wspec(gq), wspec(gkv), wspec(bg),
        pl.BlockSpec((ROPE_HALF, tm), pos_col),
        pl.BlockSpec((ROPE_HALF, tm), pos_col),
        pl.BlockSpec((tm, LANES), pos_row),
        pl.BlockSpec((tm, LANES), pos_row),
    ]
    return pl.pallas_call(
        _proj_kernel,
        grid=(T // tm,),
        in_specs=in_specs,
        out_specs=out_specs,
        out_shape=out_shapes,
        compiler_params=pltpu.CompilerParams(
            dimension_semantics=("arbitrary",), vmem_limit_bytes=VMEM_LIMIT),
        name="proj",
    )(x2, win, wqT, wk, wvT, gq, gkv, bg, cq, sq, ck, sk)


class _Head(NamedTuple):
    qT: object
    keys: object
    valsT: object
    oT: object
    bias_chunk: object


def _causal_attention(heads, s_refs):
    def qcols(i):
        return slice(i * ATT_Q, (i + 1) * ATT_Q)

    def krows(j):
        return slice(j * ATT_K, (j + 1) * ATT_K)

    def first_col(i, j):
        return max(0, j * ATT_K - i * ATT_Q)

    def widen(x, lo, fill):
        if lo == 0:
            return x
        return jnp.concatenate([jnp.full((x.shape[0], lo), fill, x.dtype), x], axis=1)

    def s_of(item):
        g, i = item
        return s_refs[g * N_Q_TILES + i]

    def scores(item, j, q_tile):
        hd, i = heads[item[0]], item[1]
        lo = first_col(i, j)
        sT = _dot(hd.keys[krows(j), :], q_tile[:, lo:])
        b = hd.bias_chunk(i, j)
        if b is not None:
            sT = sT + b[:, lo:]
        s_of(item)[krows(j), lo:] = sT
        return widen(jnp.max(sT, axis=0, keepdims=True), lo, NEG)

    def weights(item, j, m):
        hd, i = heads[item[0]], item[1]
        lo = first_col(i, j)
        p = jnp.exp2(s_of(item)[krows(j), lo:] - m[:, lo:])
        return widen(_dot(hd.valsT[:, krows(j)], p.astype(BF16)), lo, 0.0)

    def fold(op, old, new):
        return new if old is None else op(old, new)

    def n_chunks(item):
        return _n_chunks(item[1]) if item is not None else 0

    def q_of(item):
        return heads[item[0]].qT[:, qcols(item[1])] if item is not None else None

    items = [(g, i) for i in reversed(range(N_Q_TILES)) for g in range(len(heads))]
    m = []
    for item in items[:SCORE_LEAD]:
        q_tile, mi = q_of(item), None
        for j in range(n_chunks(item)):
            mi = fold(jnp.maximum, mi, scores(item, j, q_tile))
        m.append(mi)
    for pos, item in enumerate(items):
        ahead = items[pos + SCORE_LEAD] if pos + SCORE_LEAD < len(items) else None
        q_ahead, m_ahead, acc = q_of(ahead), None, None
        for j in range(max(n_chunks(item), n_chunks(ahead))):
            if j < n_chunks(item):
                acc = fold(jnp.add, acc, weights(item, j, m[pos]))
            if j < n_chunks(ahead):
                m_ahead = fold(jnp.maximum, m_ahead, scores(ahead, j, q_ahead))
        m.append(m_ahead)
        hd, i = heads[item[0]], item[1]
        d_v = hd.oT.shape[0]
        hd.oT[:, qcols(i)] = (acc[:d_v, :] / acc[d_v:d_v + 1, :]).astype(BF16)


def _n_chunks(i):
    return (i + 1) * (ATT_Q // ATT_K)


def _causal_bias(chunk_offset):
    key = lax.broadcasted_iota(jnp.int32, (ATT_K, ATT_Q), 0) + chunk_offset
    qry = lax.broadcasted_iota(jnp.int32, (ATT_K, ATT_Q), 1)
    return jnp.where(key > qry, NEG, 0.0).astype(F32)


ONES_ROWS = 16


def _values_with_ones(vT, va_ref):
    d_v = vT.shape[0]
    va_ref[0:d_v, :] = vT
    va_ref[d_v:, :] = jnp.ones((ONES_ROWS, va_ref.shape[1]), BF16)


def _att_scratch(d_v):
    return [pltpu.VMEM((HEADS_PER_STEP, d_v + ONES_ROWS, SEQ), BF16)] + [
        pltpu.VMEM((_n_chunks(i) * ATT_K, ATT_Q), F32)
        for _ in range(HEADS_PER_STEP) for i in range(N_Q_TILES)]


def _mla_kernel(qT_ref, kn_ref, kr_ref, vT_ref, oT_ref, kf_ref, va_ref, *s_refs):
    per_tile = ATT_Q // ATT_K
    masks = [_causal_bias(c * ATT_K) for c in range(per_tile)]

    def bias_chunk(i, j):
        c = j - i * per_tile
        return masks[c] if c >= 0 else None

    heads = []
    for g in range(HEADS_PER_STEP):
        qk = pl.ds(g * HEAD_PAD, HEAD_PAD)
        vo = pl.ds(g * MLA_V, MLA_V)
        kf_ref[:, qk] = kn_ref[:, qk] + kr_ref[...]
        _values_with_ones(vT_ref[vo, :], va_ref.at[g])
        heads.append(_Head(qT_ref.at[qk, :], kf_ref.at[:, qk], va_ref.at[g],
                           oT_ref.at[vo, :], bias_chunk))
    _causal_attention(heads, s_refs)


def _mla_call(qT, kn, kr, vT, batch):
    T = kn.shape[0]
    g = HEADS_PER_STEP
    return pl.pallas_call(
        _mla_kernel,
        grid=(batch, MLA_HEADS // g),
        in_specs=[
            pl.BlockSpec((g * HEAD_PAD, SEQ), lambda b, h: (h, b)),
            pl.BlockSpec((SEQ, g * HEAD_PAD), lambda b, h: (b, h)),
            pl.BlockSpec((SEQ, LANES), lambda b, h: (b, 0)),
            pl.BlockSpec((g * MLA_V, SEQ), lambda b, h: (h, b)),
        ],
        out_specs=pl.BlockSpec((g * MLA_V, SEQ), lambda b, h: (h, b)),
        out_shape=jax.ShapeDtypeStruct((MLA_WIDTH, T), BF16),
        scratch_shapes=[pltpu.VMEM((SEQ, g * HEAD_PAD), BF16)] + _att_scratch(MLA_V),
        compiler_params=pltpu.CompilerParams(
            dimension_semantics=("arbitrary", "arbitrary"), vmem_limit_bytes=VMEM_LIMIT),
        name="mla_attn",
    )(qT, kn, kr, vT)


N_DIL_OFFSETS = SEQ // ATT_K + 1


def _dil_kernel(qT_ref, k_ref, vT_ref, bias_ref, oT_ref, q2_ref, va_ref, *s_refs):
    per_tile = ATT_Q // ATT_K
    d = DIL_HEAD_DIM
    heads = []
    for g in range(HEADS_PER_STEP):
        rows = pl.ds(g * d, d)
        q2_ref[g] = jnp.zeros(q2_ref.shape[1:], BF16)
        q2_ref[g, rows, :] = qT_ref[rows, :]
        _values_with_ones(vT_ref[rows, :], va_ref.at[g])

        def bias_chunk(i, j, g=g):
            return jnp.concatenate(
                [bias_ref[g, i * per_tile + c - j + 1] for c in range(per_tile)], axis=1)

        heads.append(_Head(q2_ref.at[g], k_ref, va_ref.at[g], oT_ref.at[rows, :], bias_chunk))
    _causal_attention(heads, s_refs)


def _dil_call(qdT, kd, vdT, bias, batch):
    T = kd.shape[0]
    g = HEADS_PER_STEP
    assert g * DIL_HEAD_DIM == LANES
    return pl.pallas_call(
        _dil_kernel,
        grid=(DIL_HEADS // g, batch),
        in_specs=[
            pl.BlockSpec((LANES, SEQ), lambda h, b: (h, b)),
            pl.BlockSpec((SEQ, LANES), lambda h, b: (b, h)),
            pl.BlockSpec((LANES, SEQ), lambda h, b: (h, b)),
            pl.BlockSpec((g, N_DIL_OFFSETS, ATT_K, ATT_K), lambda h, b: (h, 0, 0, 0)),
        ],
        out_specs=pl.BlockSpec((LANES, SEQ), lambda h, b: (h, b)),
        out_shape=jax.ShapeDtypeStruct((DIL_WIDTH, T), BF16),
        scratch_shapes=[pltpu.VMEM((g, LANES, SEQ), BF16)] + _att_scratch(DIL_HEAD_DIM),
        compiler_params=pltpu.CompilerParams(
            dimension_semantics=("arbitrary", "arbitrary"), vmem_limit_bytes=VMEM_LIMIT),
        name="dil_attn",
    )(qdT, kd, vdT, bias)


def _dilated_bias():
    off = np.arange(-1, N_DIL_OFFSETS - 1)[:, None, None]
    key = np.arange(ATT_K)[None, :, None]
    qry = np.arange(ATT_K)[None, None, :]
    dist = ATT_K * off + qry - key
    count = np.zeros(dist.shape, np.int32)
    for window, dilation in DIL_PATTERNS:
        count += ((dist >= 0) & (dist % dilation == 0) & (dist <= window)).astype(np.int32)
    slopes = jnp.asarray([2.0 ** (-8.0 * (i + 1) / DIL_HEADS) for i in range(DIL_HEADS)], F32)
    logc = jnp.asarray(np.log2(np.maximum(count, 1)), F32)
    alibi = -(slopes * LOG2E)[:, None, None, None] * jnp.asarray(dist, F32)[None]
    return jnp.where(jnp.asarray(count > 0)[None], alibi + logc[None], NEG)


def _post_kernel(alpha, x_ref, oaT_ref, obT_ref, g_ref, woa_ref, wob_ref, wout_ref,
                 l1g_ref, l1b_ref, w1_ref, w2_ref, l2g_ref, l2b_ref, out_ref):
    ya = _dot(oaT_ref[...].T, woa_ref[...])
    yb = _dot(obT_ref[...].T, wob_ref[...])
    g = g_ref[...].astype(F32)
    mix = (g[:, :D_MODEL] * ya + g[:, D_MODEL:] * yb).astype(BF16)
    mixed = _dot(mix, wout_ref[...])
    h = _layer_norm(alpha * x_ref[...] + mixed, l1g_ref[...], l1b_ref[...])
    hb = h.astype(BF16)
    f = jnp.zeros_like(h)
    for c in range(D_FF // FF_CHUNK):
        u = jnp.maximum(_dot(hb, w1_ref[:, c * FF_CHUNK:(c + 1) * FF_CHUNK]), 0.0)
        f = f + _dot((u * u).astype(BF16), w2_ref[c * FF_CHUNK:(c + 1) * FF_CHUNK, :])
    out_ref[...] = _layer_norm(alpha * h + f, l2g_ref[...], l2b_ref[...])


def _post_call(alpha, x2, oaT, obT, g, woa, wob, wout, l1g, l1b, w1, w2, l2g, l2b):
    T = x2.shape[0]
    tm = ROW_TILE
    const = lambda i: (0, 0)
    row = lambda i: (i, 0)
    col = lambda i: (0, i)

    def wspec(a):
        return pl.BlockSpec(a.shape, const, pipeline_mode=pl.Buffered(1))

    return pl.pallas_call(
        functools.partial(_post_kernel, alpha),
        grid=(T // tm,),
        in_specs=[
            pl.BlockSpec((tm, D_MODEL), row),
            pl.BlockSpec((MLA_WIDTH, tm), col),
            pl.BlockSpec((DIL_WIDTH, tm), col),
            pl.BlockSpec((tm, N_BRANCH * D_MODEL), row),
            wspec(woa), wspec(wob), wspec(wout), wspec(l1g), wspec(l1b),
            wspec(w1), wspec(w2), wspec(l2g), wspec(l2b),
        ],
        out_specs=pl.BlockSpec((tm, D_MODEL), row),
        out_shape=jax.ShapeDtypeStruct((T, D_MODEL), F32),
        compiler_params=pltpu.CompilerParams(
            dimension_semantics=("arbitrary",), vmem_limit_bytes=VMEM_LIMIT),
        name="post",
    )(x2, oaT, obT, g, woa, wob, wout, l1g, l1b, w1, w2, l2g, l2b)


def _rotate_half_cols(w):
    half = w.shape[-1] // 2
    return jnp.concatenate([-w[..., half:], w[..., :half]], axis=-1)


def _rope_tables():
    inv = jnp.power(ROPE_THETA, -jnp.arange(ROPE_HALF, dtype=F32) / ROPE_HALF)
    ang = jnp.arange(SEQ).astype(F32)[:, None] * inv[None, :]
    cos, sin = jnp.cos(ang), jnp.sin(ang)
    qscale = (MLA_QK ** -0.5) * LOG2E
    cq = (cos * qscale).T
    sq = (sin * qscale).T
    zeros = jnp.zeros((SEQ, MLA_NOPE), F32)
    z32 = jnp.zeros((SEQ, MLA_ROPE), F32)
    ck = jnp.concatenate([zeros, cos, cos, z32], axis=1)
    sk = jnp.concatenate([zeros, z32, sin, sin], axis=1)
    return cq, sq, ck, sk


def _layer(x2, batch, alpha, w_in, b_gate, g_q_a, w_uq, g_kv_a, w_ukv, w_o_mla, w_o_dil,
           w_out, ln1_g, ln1_b, w_ff1, w_ff2, ln2_g, ln2_b):
    s0, s1, s2, s3 = Q_LORA, Q_LORA + KV_LORA, Q_LORA + KV_LORA + MLA_ROPE, \
        Q_LORA + KV_LORA + MLA_ROPE + 3 * DIL_WIDTH
    w_kr = w_in[:, s1:s2]
    win = jnp.concatenate(
        [w_in[:, :s1], jnp.zeros((D_MODEL, MLA_NOPE), F32), w_kr, _rotate_half_cols(w_kr),
         w_in[:, s2:s3], w_in[:, s3:]], axis=1).astype(BF16)

    wq = w_uq.reshape(Q_LORA, MLA_HEADS, MLA_QK)
    wq = jnp.pad(wq, ((0, 0), (0, 0), (0, HEAD_PAD - MLA_QK)))
    wqT = wq.reshape(Q_LORA, MLA_HEADS * HEAD_PAD).T.astype(BF16)
    wkv = w_ukv.reshape(KV_LORA, MLA_HEADS, MLA_NOPE + MLA_V)
    wk = jnp.pad(wkv[:, :, :MLA_NOPE], ((0, 0), (0, 0), (0, HEAD_PAD - MLA_NOPE)))
    wk = wk.reshape(KV_LORA, MLA_HEADS * HEAD_PAD).astype(BF16)
    wvT = wkv[:, :, MLA_NOPE:].reshape(KV_LORA, MLA_WIDTH).T.astype(BF16)

    cq, sq, ck, sk = _rope_tables()
    qT, kn, kr, vT, qdT, kd, vdT, g = _proj_call(
        x2, win, wqT, wk, wvT, g_q_a[None], g_kv_a[None],
        b_gate.reshape(1, N_BRANCH * D_MODEL), cq, sq, ck, sk)

    oaT = _mla_call(qT, kn, kr, vT, batch)
    obT = _dil_call(qdT, kd, vdT, _dilated_bias(), batch)

    return _post_call(alpha, x2, oaT, obT, g, w_o_mla.astype(BF16), w_o_dil.astype(BF16),
                      w_out.astype(BF16), ln1_g[None], ln1_b[None], w_ff1.astype(BF16),
                      w_ff2.astype(BF16), ln2_g[None], ln2_b[None])


def kernel(x, w_in, b_gate, g_q_a, w_uq, g_kv_a, w_ukv, w_o_mla, w_o_dil, w_out,
           ln1_g, ln1_b, w_ff1, w_ff2, ln2_g, ln2_b):
    batch, seq, d = x.shape
    assert (seq, d) == (SEQ, D_MODEL)
    depth = w_in.shape[0]
    alpha = (2 * depth) ** 0.25
    x2 = x.reshape(batch * seq, d)
    for l in range(depth):
        x2 = _layer(x2, batch, alpha, w_in[l], b_gate[l], g_q_a[l], w_uq[l], g_kv_a[l],
                    w_ukv[l], w_o_mla[l], w_o_dil[l], w_out[l], ln1_g[l], ln1_b[l],
                    w_ff1[l], w_ff2[l], ln2_g[l], ln2_b[l])
    return x2.reshape(batch, seq, d)
```

```python
import functools
import math
from typing import NamedTuple

import jax
import jax.numpy as jnp
import numpy as np
from jax import lax
from jax.experimental import pallas as pl
from jax.experimental.pallas import tpu as pltpu

D_MODEL = 1024
SEQ = 2048

MLA_HEADS = 8
MLA_NOPE = 64
MLA_ROPE = 32
MLA_V = 64
Q_LORA = 384
KV_LORA = 256
ROPE_THETA = 10000.0
MLA_QK = MLA_NOPE + MLA_ROPE
MLA_WIDTH = MLA_HEADS * MLA_V
ROPE_HALF = MLA_ROPE // 2

DIL_HEADS = 8
DIL_HEAD_DIM = 64
DIL_PATTERNS = ((128, 1), (512, 4), (2048, 16))
DIL_WIDTH = DIL_HEADS * DIL_HEAD_DIM

N_BRANCH = 2
D_FF = 4 * D_MODEL
LN_EPS = 1e-5
RMS_EPS = 1e-6
NEG = -1e30

LOG2E = math.log2(math.e)

LANES = 128
MXU_DIM = 256
VMEM_LIMIT = 56 * 1024 * 1024

HEAD_PAD = LANES
ROW_TILE = 512
ATT_Q = 2 * MXU_DIM
ATT_K = MXU_DIM
N_Q_TILES = SEQ // ATT_Q
HEADS_PER_STEP = 4
HEAD_GROUP = 2
SCORE_LEAD = 2
FF_CHUNK = 1024

C_QA = 0
C_KVA = C_QA + Q_LORA
C_KR = C_KVA + KV_LORA
C_QD = C_KR + LANES
C_KD = C_QD + DIL_WIDTH
C_VD = C_KD + DIL_WIDTH
C_GATE = C_VD + DIL_WIDTH
IN_AUG = C_GATE + N_BRANCH * D_MODEL

BF16 = jnp.bfloat16
F32 = jnp.float32


def _dot(a, b):
    return jnp.dot(a, b, preferred_element_type=F32)


def _dot_nt(a, b):
    return lax.dot_general(a, b, (((1,), (1,)), ((), ())), preferred_element_type=F32)


def _rms(t, g):
    return t * lax.rsqrt(jnp.mean(t * t, axis=-1, keepdims=True) + RMS_EPS) * g


def _layer_norm(t, g, b):
    mu = jnp.mean(t, axis=-1, keepdims=True)
    d = t - mu
    var = jnp.mean(d * d, axis=-1, keepdims=True)
    return d * lax.rsqrt(var + LN_EPS) * g + b


def _proj_kernel(x_ref, win_ref, wqT_ref, wk_ref, wvT_ref, gq_ref, gkv_ref, bg_ref,
                 cq_ref, sq_ref, ck_ref, sk_ref,
                 qT_ref, kn_ref, kr_ref, vT_ref, qdT_ref, kd_ref, vdT_ref, g_ref):
    xb = x_ref[...].astype(BF16)

    lat = _dot_nt(xb, win_ref[C_QA:C_QD, :])
    qn = _rms(lat[:, C_QA:C_KVA], gq_ref[...])
    kvn = _rms(lat[:, C_KVA:C_KR], gkv_ref[...])
    qnT = qn.T.astype(BF16)
    kvnT = kvn.T.astype(BF16)

    qT = _dot(wqT_ref[...], qnT)
    cq = cq_ref[...]
    sq = sq_ref[...]
    qscale = (MLA_QK ** -0.5) * LOG2E
    for h in range(MLA_HEADS):
        b = h * HEAD_PAD
        qT_ref[b:b + MLA_NOPE, :] = (qT[b:b + MLA_NOPE, :] * qscale).astype(BF16)
        t1 = qT[b + MLA_NOPE:b + MLA_NOPE + ROPE_HALF, :]
        t2 = qT[b + MLA_NOPE + ROPE_HALF:b + MLA_QK, :]
        qT_ref[b + MLA_NOPE:b + MLA_NOPE + ROPE_HALF, :] = (t1 * cq - t2 * sq).astype(BF16)
        qT_ref[b + MLA_NOPE + ROPE_HALF:b + MLA_QK, :] = (t1 * sq + t2 * cq).astype(BF16)
        qT_ref[b + MLA_QK:b + HEAD_PAD, :] = jnp.zeros((HEAD_PAD - MLA_QK, qT.shape[1]), BF16)

    kn_ref[...] = _dot(kvn.astype(BF16), wk_ref[...]).astype(BF16)
    vT_ref[...] = _dot(wvT_ref[...], kvnT).astype(BF16)

    dil = _dot_nt(xb, win_ref[C_QD:C_GATE, :])
    dscale = (DIL_HEAD_DIM ** -0.5) * LOG2E
    qdT_ref[...] = (dil[:, :DIL_WIDTH] * dscale).T.astype(BF16)
    kd_ref[...] = dil[:, DIL_WIDTH:2 * DIL_WIDTH].astype(BF16)
    vdT_ref[...] = dil[:, 2 * DIL_WIDTH:].T.astype(BF16)

    gates = _dot_nt(xb, win_ref[C_GATE:IN_AUG, :]) + bg_ref[...]
    g_ref[...] = jax.nn.sigmoid(gates).astype(BF16)

    kr = lat[:, C_KR:C_QD]
    kr = kr * ck_ref[...] + pltpu.roll(kr * sk_ref[...], LANES - MLA_ROPE, axis=1)
    kr_ref[...] = kr.astype(BF16)


def _proj_call(x2, win, wqT, wk, wvT, gq, gkv, bg, cq, sq, ck, sk):
    T = x2.shape[0]
    tm = ROW_TILE
    tiles_per_seq = SEQ // tm
    const = lambda i: (0, 0)
    row = lambda i: (i, 0)
    col = lambda i: (0, i)
    pos_col = lambda i: (0, i % tiles_per_seq)
    pos_row = lambda i: (i % tiles_per_seq, 0)

    def wspec(a):
        return pl.BlockSpec(a.shape, const, pipeline_mode=pl.Buffered(1))

    out_shapes = (
        jax.ShapeDtypeStruct((MLA_HEADS * HEAD_PAD, T), BF16),
        jax.ShapeDtypeStruct((T, MLA_HEADS * HEAD_PAD), BF16),
        jax.ShapeDtypeStruct((T, LANES), BF16),
        jax.ShapeDtypeStruct((MLA_WIDTH, T), BF16),
        jax.ShapeDtypeStruct((DIL_WIDTH, T), BF16),
        jax.ShapeDtypeStruct((T, DIL_WIDTH), BF16),
        jax.ShapeDtypeStruct((DIL_WIDTH, T), BF16),
        jax.ShapeDtypeStruct((T, N_BRANCH * D_MODEL), BF16),
    )
    out_specs = (
        pl.BlockSpec((MLA_HEADS * HEAD_PAD, tm), col),
        pl.BlockSpec((tm, MLA_HEADS * HEAD_PAD), row),
        pl.BlockSpec((tm, LANES), row),
        pl.BlockSpec((MLA_WIDTH, tm), col),
        pl.BlockSpec((DIL_WIDTH, tm), col),
        pl.BlockSpec((tm, DIL_WIDTH), row),
        pl.BlockSpec((DIL_WIDTH, tm), col),
        pl.BlockSpec((tm, N_BRANCH * D_MODEL), row),
    )
    in_specs = [
        pl.BlockSpec((tm, D_MODEL), row),
        wspec(win), wspec(wqT), wspec(wk), wspec(wvT), wspec(gq), wspec(gkv), wspec(bg),
        pl.BlockSpec((ROPE_HALF, tm), pos_col),
        pl.BlockSpec((ROPE_HALF, tm), pos_col),
        pl.BlockSpec((tm, LANES), pos_row),
        pl.BlockSpec((tm, LANES), pos_row),
    ]
    return pl.pallas_call(
        _proj_kernel,
        grid=(T // tm,),
        in_specs=in_specs,
        out_specs=out_specs,
        out_shape=out_shapes,
        compiler_params=pltpu.CompilerParams(
            dimension_semantics=("arbitrary",), vmem_limit_bytes=VMEM_LIMIT),
        name="proj",
    )(x2, win, wqT, wk, wvT, gq, gkv, bg, cq, sq, ck, sk)


class _Head(NamedTuple):
    qT: object
    keys: object
    valsT: object
    oT: object
    bias_chunk: object


def _causal_attention(heads, s_refs):
    def qcols(i):
        return slice(i * ATT_Q, (i + 1) * ATT_Q)

    def krows(j):
        return slice(j * ATT_K, (j + 1) * ATT_K)

    def first_col(i, j):
        return max(0, j * ATT_K - i * ATT_Q)

    def widen(x, lo, fill):
        if lo == 0:
            return x
        return jnp.concatenate([jnp.full((x.shape[0], lo), fill, x.dtype), x], axis=1)

    def s_of(item):
        g, i = item
        return s_refs[(g % HEAD_GROUP) * N_Q_TILES + i]

    def scores(item, j, q_tile):
        hd, i = heads[item[0]], item[1]
        lo = first_col(i, j)
        sT = _dot(hd.keys[krows(j), :], q_tile[:, lo:])
        b = hd.bias_chunk(i, j)
        if b is not None:
            sT = sT + b[:, lo:]
        s_of(item)[krows(j), lo:] = sT
        return widen(jnp.max(sT, axis=0, keepdims=True), lo, NEG)

    def weights(item, j, m):
        hd, i = heads[item[0]], item[1]
        lo = first_col(i, j)
        p = jnp.exp2(s_of(item)[krows(j), lo:] - m[:, lo:])
        return widen(_dot(hd.valsT[:, krows(j)], p.astype(BF16)), lo, 0.0)

    def fold(op, old, new):
        return new if old is None else op(old, new)

    def n_chunks(item):
        return _n_chunks(item[1]) if item is not None else 0

    def q_of(item):
        return heads[item[0]].qT[:, qcols(item[1])] if item is not None else None

    items = [(g, i) for base in range(0, len(heads), HEAD_GROUP)
             for i in reversed(range(N_Q_TILES)) for g in range(base, base + HEAD_GROUP)]
    m = []
    for item in items[:SCORE_LEAD]:
        q_tile, mi = q_of(item), None
        for j in range(n_chunks(item)):
            mi = fold(jnp.maximum, mi, scores(item, j, q_tile))
        m.append(mi)
    for pos, item in enumerate(items):
        ahead = items[pos + SCORE_LEAD] if pos + SCORE_LEAD < len(items) else None
        q_ahead, m_ahead, acc = q_of(ahead), None, None
        for j in range(max(n_chunks(item), n_chunks(ahead))):
            if j < n_chunks(item):
                acc = fold(jnp.add, acc, weights(item, j, m[pos]))
            if j < n_chunks(ahead):
                m_ahead = fold(jnp.maximum, m_ahead, scores(ahead, j, q_ahead))
        m.append(m_ahead)
        hd, i = heads[item[0]], item[1]
        d_v = hd.oT.shape[0]
        hd.oT[:, qcols(i)] = (acc[:d_v, :] / acc[d_v:d_v + 1, :]).astype(BF16)


def _n_chunks(i):
    return (i + 1) * (ATT_Q // ATT_K)


def _causal_bias(chunk_offset):
    key = lax.broadcasted_iota(jnp.int32, (ATT_K, ATT_Q), 0) + chunk_offset
    qry = lax.broadcasted_iota(jnp.int32, (ATT_K, ATT_Q), 1)
    return jnp.where(key > qry, NEG, 0.0).astype(F32)


ONES_ROWS = 16


def _values_with_ones(vT, va_ref):
    d_v = vT.shape[0]
    va_ref[0:d_v, :] = vT
    va_ref[d_v:, :] = jnp.ones((ONES_ROWS, va_ref.shape[1]), BF16)


def _att_scratch(d_v):
    return [pltpu.VMEM((HEADS_PER_STEP, d_v + ONES_ROWS, SEQ), BF16)] + [
        pltpu.VMEM((_n_chunks(i) * ATT_K, ATT_Q), F32)
        for _ in range(HEAD_GROUP) for i in range(N_Q_TILES)]


def _mla_kernel(qT_ref, kn_ref, kr_ref, vT_ref, oT_ref, kf_ref, va_ref, *s_refs):
    per_tile = ATT_Q // ATT_K
    masks = [_causal_bias(c * ATT_K) for c in range(per_tile)]

    def bias_chunk(i, j):
        c = j - i * per_tile
        return masks[c] if c >= 0 else None

    heads = []
    for g in range(HEADS_PER_STEP):
        qk = pl.ds(g * HEAD_PAD, HEAD_PAD)
        vo = pl.ds(g * MLA_V, MLA_V)
        kf_ref[:, qk] = kn_ref[:, qk] + kr_ref[...]
        _values_with_ones(vT_ref[vo, :], va_ref.at[g])
        heads.append(_Head(qT_ref.at[qk, :], kf_ref.at[:, qk], va_ref.at[g],
                           oT_ref.at[vo, :], bias_chunk))
    _causal_attention(heads, s_refs)


def _mla_call(qT, kn, kr, vT, batch):
    T = kn.shape[0]
    g = HEADS_PER_STEP
    return pl.pallas_call(
        _mla_kernel,
        grid=(batch, MLA_HEADS // g),
        in_specs=[
            pl.BlockSpec((g * HEAD_PAD, SEQ), lambda b, h: (h, b)),
            pl.BlockSpec((SEQ, g * HEAD_PAD), lambda b, h: (b, h)),
            pl.BlockSpec((SEQ, LANES), lambda b, h: (b, 0)),
            pl.BlockSpec((g * MLA_V, SEQ), lambda b, h: (h, b)),
        ],
        out_specs=pl.BlockSpec((g * MLA_V, SEQ), lambda b, h: (h, b)),
        out_shape=jax.ShapeDtypeStruct((MLA_WIDTH, T), BF16),
        scratch_shapes=[pltpu.VMEM((SEQ, g * HEAD_PAD), BF16)] + _att_scratch(MLA_V),
        compiler_params=pltpu.CompilerParams(
            dimension_semantics=("arbitrary", "arbitrary"), vmem_limit_bytes=VMEM_LIMIT),
        name="mla_attn",
    )(qT, kn, kr, vT)


N_DIL_OFFSETS = SEQ // ATT_K + 1


def _dil_kernel(slopes_ref, qT_ref, k_ref, vT_ref, dist_ref, logc_ref, oT_ref,
                q2_ref, bias_ref, va_ref, *s_refs):
    per_tile = ATT_Q // ATT_K
    d = DIL_HEAD_DIM

    @pl.when(pl.program_id(1) == 0)
    def _():
        for g in range(HEADS_PER_STEP):
            slope = slopes_ref[pl.program_id(0) * HEADS_PER_STEP + g]
            bias_ref[g] = logc_ref[...] - slope * dist_ref[...]

    heads = []
    for g in range(HEADS_PER_STEP):
        rows = pl.ds(g * d, d)
        pair = pl.ds((g * d // LANES) * LANES, LANES)
        q2_ref[g] = jnp.zeros(q2_ref.shape[1:], BF16)
        q2_ref[g, pl.ds(g * d % LANES, d), :] = qT_ref[rows, :]
        _values_with_ones(vT_ref[rows, :], va_ref.at[g])

        def bias_chunk(i, j, g=g):
            return jnp.concatenate(
                [bias_ref[g, i * per_tile + c - j + 1] for c in range(per_tile)], axis=1)

        heads.append(_Head(q2_ref.at[g], k_ref.at[:, pair], va_ref.at[g],
                           oT_ref.at[rows, :], bias_chunk))
    _causal_attention(heads, s_refs)


def _dil_call(qdT, kd, vdT, batch):
    T = kd.shape[0]
    g = HEADS_PER_STEP
    width = g * DIL_HEAD_DIM
    assert width % LANES == 0
    slopes = jnp.asarray(
        [LOG2E * 2.0 ** (-8.0 * (i + 1) / DIL_HEADS) for i in range(DIL_HEADS)], F32)
    dist, logc = _dilated_tables()
    table = pl.BlockSpec(dist.shape, lambda h, b: (0, 0, 0), pipeline_mode=pl.Buffered(1))
    return pl.pallas_call(
        _dil_kernel,
        grid=(DIL_HEADS // g, batch),
        in_specs=[
            pl.BlockSpec(memory_space=pltpu.SMEM),
            pl.BlockSpec((width, SEQ), lambda h, b: (h, b)),
            pl.BlockSpec((SEQ, width), lambda h, b: (b, h)),
            pl.BlockSpec((width, SEQ), lambda h, b: (h, b)),
            table, table,
        ],
        out_specs=pl.BlockSpec((width, SEQ), lambda h, b: (h, b)),
        out_shape=jax.ShapeDtypeStruct((DIL_WIDTH, T), BF16),
        scratch_shapes=[pltpu.VMEM((g, LANES, SEQ), BF16),
                        pltpu.VMEM((g, N_DIL_OFFSETS, ATT_K, ATT_K), F32)]
        + _att_scratch(DIL_HEAD_DIM),
        compiler_params=pltpu.CompilerParams(
            dimension_semantics=("arbitrary", "arbitrary"), vmem_limit_bytes=VMEM_LIMIT),
        name="dil_attn",
    )(slopes, qdT, kd, vdT, dist, logc)


def _dilated_tables():
    off = np.arange(-1, N_DIL_OFFSETS - 1)[:, None, None]
    key = np.arange(ATT_K)[None, :, None]
    qry = np.arange(ATT_K)[None, None, :]
    dist = ATT_K * off + qry - key
    count = np.zeros(dist.shape, np.int32)
    for window, dilation in DIL_PATTERNS:
        count += ((dist >= 0) & (dist % dilation == 0) & (dist <= window)).astype(np.int32)
    logc = np.where(count > 0, np.log2(np.maximum(count, 1)), NEG)
    return jnp.asarray(dist, F32), jnp.asarray(logc, F32)


def _post_kernel(alpha, x_ref, oaT_ref, obT_ref, g_ref, woa_ref, wob_ref, wout_ref,
                 l1g_ref, l1b_ref, w1_ref, w2_ref, l2g_ref, l2b_ref, out_ref):
    ya = _dot(oaT_ref[...].T, woa_ref[...])
    yb = _dot(obT_ref[...].T, wob_ref[...])
    g = g_ref[...].astype(F32)
    mix = (g[:, :D_MODEL] * ya + g[:, D_MODEL:] * yb).astype(BF16)
    mixed = _dot(mix, wout_ref[...])
    h = _layer_norm(alpha * x_ref[...] + mixed, l1g_ref[...], l1b_ref[...])
    hb = h.astype(BF16)
    f = jnp.zeros_like(h)
    for c in range(D_FF // FF_CHUNK):
        u = jnp.maximum(_dot(hb, w1_ref[:, c * FF_CHUNK:(c + 1) * FF_CHUNK]), 0.0)
        f = f + _dot((u * u).astype(BF16), w2_ref[c * FF_CHUNK:(c + 1) * FF_CHUNK, :])
    out_ref[...] = _layer_norm(alpha * h + f, l2g_ref[...], l2b_ref[...])


def _post_call(alpha, x2, oaT, obT, g, woa, wob, wout, l1g, l1b, w1, w2, l2g, l2b):
    T = x2.shape[0]
    tm = ROW_TILE
    const = lambda i: (0, 0)
    row = lambda i: (i, 0)
    col = lambda i: (0, i)

    def wspec(a):
        return pl.BlockSpec(a.shape, const, pipeline_mode=pl.Buffered(1))

    return pl.pallas_call(
        functools.partial(_post_kernel, alpha),
        grid=(T // tm,),
        in_specs=[
            pl.BlockSpec((tm, D_MODEL), row),
            pl.BlockSpec((MLA_WIDTH, tm), col),
            pl.BlockSpec((DIL_WIDTH, tm), col),
            pl.BlockSpec((tm, N_BRANCH * D_MODEL), row),
            wspec(woa), wspec(wob), wspec(wout), wspec(l1g), wspec(l1b),
            wspec(w1), wspec(w2), wspec(l2g), wspec(l2b),
        ],
        out_specs=pl.BlockSpec((tm, D_MODEL), row),
        out_shape=jax.ShapeDtypeStruct((T, D_MODEL), F32),
        compiler_params=pltpu.CompilerParams(
            dimension_semantics=("arbitrary",), vmem_limit_bytes=VMEM_LIMIT),
        name="post",
    )(x2, oaT, obT, g, woa, wob, wout, l1g, l1b, w1, w2, l2g, l2b)


def _rotate_half_rows(w):
    half = w.shape[0] // 2
    return jnp.concatenate([-w[half:], w[:half]], axis=0)


def _rope_tables():
    inv = jnp.power(ROPE_THETA, -jnp.arange(ROPE_HALF, dtype=F32) / ROPE_HALF)
    ang = jnp.arange(SEQ).astype(F32)[:, None] * inv[None, :]
    cos, sin = jnp.cos(ang), jnp.sin(ang)
    qscale = (MLA_QK ** -0.5) * LOG2E
    cq = (cos * qscale).T
    sq = (sin * qscale).T
    zeros = jnp.zeros((SEQ, MLA_NOPE), F32)
    z32 = jnp.zeros((SEQ, MLA_ROPE), F32)
    ck = jnp.concatenate([zeros, cos, cos, z32], axis=1)
    sk = jnp.concatenate([zeros, z32, sin, sin], axis=1)
    return cq, sq, ck, sk


def _layer(x2, batch, alpha, w_in, b_gate, g_q_a, w_uq, g_kv_a, w_ukv, w_o_mla, w_o_dil,
           w_out, ln1_g, ln1_b, w_ff1, w_ff2, ln2_g, ln2_b):
    s0, s1, s2, s3 = Q_LORA, Q_LORA + KV_LORA, Q_LORA + KV_LORA + MLA_ROPE, \
        Q_LORA + KV_LORA + MLA_ROPE + 3 * DIL_WIDTH
    w_inT = w_in.T.astype(BF16)
    w_kr = w_inT[s1:s2]
    win = jnp.concatenate(
        [w_inT[:s1], jnp.zeros((MLA_NOPE, D_MODEL), BF16), w_kr, _rotate_half_rows(w_kr),
         w_inT[s2:]], axis=0)

    wq = w_uq.reshape(Q_LORA, MLA_HEADS, MLA_QK)
    wq = jnp.pad(wq, ((0, 0), (0, 0), (0, HEAD_PAD - MLA_QK)))
    wqT = wq.reshape(Q_LORA, MLA_HEADS * HEAD_PAD).T.astype(BF16)
    wkv = w_ukv.reshape(KV_LORA, MLA_HEADS, MLA_NOPE + MLA_V)
    wk = jnp.pad(wkv[:, :, :MLA_NOPE], ((0, 0), (0, 0), (0, HEAD_PAD - MLA_NOPE)))
    wk = wk.reshape(KV_LORA, MLA_HEADS * HEAD_PAD).astype(BF16)
    wvT = wkv[:, :, MLA_NOPE:].reshape(KV_LORA, MLA_WIDTH).T.astype(BF16)

    cq, sq, ck, sk = _rope_tables()
    qT, kn, kr, vT, qdT, kd, vdT, g = _proj_call(
        x2, win, wqT, wk, wvT, g_q_a[None], g_kv_a[None],
        b_gate.reshape(1, N_BRANCH * D_MODEL), cq, sq, ck, sk)

    oaT = _mla_call(qT, kn, kr, vT, batch)
    obT = _dil_call(qdT, kd, vdT, batch)

    return _post_call(alpha, x2, oaT, obT, g, w_o_mla.astype(BF16), w_o_dil.astype(BF16),
                      w_out.astype(BF16), ln1_g[None], ln1_b[None], w_ff1.astype(BF16),
                      w_ff2.astype(BF16), ln2_g[None], ln2_b[None])


def kernel(x, w_in, b_gate, g_q_a, w_uq, g_kv_a, w_ukv, w_o_mla, w_o_dil, w_out,
           ln1_g, ln1_b, w_ff1, w_ff2, ln2_g, ln2_b):
    batch, seq, d = x.shape
    assert (seq, d) == (SEQ, D_MODEL)
    depth = w_in.shape[0]
    alpha = (2 * depth) ** 0.25
    x2 = x.reshape(batch * seq, d)
    for l in range(depth):
        x2 = _layer(x2, batch, alpha, w_in[l], b_gate[l], g_q_a[l], w_uq[l], g_kv_a[l],
                    w_ukv[l], w_o_mla[l], w_o_dil[l], w_out[l], ln1_g[l], ln1_b[l],
                    w_ff1[l], w_ff2[l], ln2_g[l], ln2_b[l])
    return x2.reshape(batch, seq, d)
```

```python
import functools
import math
from typing import NamedTuple

import jax
import jax.numpy as jnp
import numpy as np
from jax import lax
from jax.experimental import pallas as pl
from jax.experimental.pallas import tpu as pltpu

D_MODEL = 1024
SEQ = 2048

MLA_HEADS = 8
MLA_NOPE = 64
MLA_ROPE = 32
MLA_V = 64
Q_LORA = 384
KV_LORA = 256
ROPE_THETA = 10000.0
MLA_QK = MLA_NOPE + MLA_ROPE
MLA_WIDTH = MLA_HEADS * MLA_V
ROPE_HALF = MLA_ROPE // 2

DIL_HEADS = 8
DIL_HEAD_DIM = 64
DIL_PATTERNS = ((128, 1), (512, 4), (2048, 16))
DIL_WIDTH = DIL_HEADS * DIL_HEAD_DIM

N_BRANCH = 2
D_FF = 4 * D_MODEL
LN_EPS = 1e-5
RMS_EPS = 1e-6
NEG = -1e30

LOG2E = math.log2(math.e)

LANES = 128
MXU_DIM = 256
VMEM_LIMIT = 56 * 1024 * 1024

HEAD_PAD = LANES
ROW_TILE = 512
ATT_Q = 2 * MXU_DIM
ATT_K = MXU_DIM
N_Q_TILES = SEQ // ATT_Q
HEADS_PER_STEP = 4
HEAD_GROUP = 2
SCORE_LEAD = 2
FF_CHUNK = 1024
POST_ROW_GROUPS = 2

C_QA = 0
C_KVA = C_QA + Q_LORA
C_KR = C_KVA + KV_LORA
C_QD = C_KR + LANES
C_KD = C_QD + DIL_WIDTH
C_VD = C_KD + DIL_WIDTH
C_GATE = C_VD + DIL_WIDTH
IN_AUG = C_GATE + N_BRANCH * D_MODEL

BF16 = jnp.bfloat16
F32 = jnp.float32


def _dot(a, b):
    return jnp.dot(a, b, preferred_element_type=F32)


def _dot_nt(a, b):
    return lax.dot_general(a, b, (((1,), (1,)), ((), ())), preferred_element_type=F32)


def _rms(t, g):
    return t * lax.rsqrt(jnp.mean(t * t, axis=-1, keepdims=True) + RMS_EPS) * g


def _layer_norm(t, g, b):
    mu = jnp.mean(t, axis=-1, keepdims=True)
    d = t - mu
    var = jnp.mean(d * d, axis=-1, keepdims=True)
    return d * lax.rsqrt(var + LN_EPS) * g + b


def _proj_kernel(x_ref, win_ref, wqT_ref, wk_ref, wvT_ref, gq_ref, gkv_ref, bg_ref,
                 cq_ref, sq_ref, ck_ref, sk_ref,
                 qT_ref, kn_ref, kr_ref, vT_ref, qdT_ref, kd_ref, vdT_ref, g_ref):
    xb = x_ref[...].astype(BF16)

    lat = _dot_nt(xb, win_ref[C_QA:C_QD, :])
    qn = _rms(lat[:, C_QA:C_KVA], gq_ref[...])
    kvn = _rms(lat[:, C_KVA:C_KR], gkv_ref[...])
    qnT = qn.T.astype(BF16)
    kvnT = kvn.T.astype(BF16)

    qT = _dot(wqT_ref[...], qnT)
    cq = cq_ref[...]
    sq = sq_ref[...]
    qscale = (MLA_QK ** -0.5) * LOG2E
    for h in range(MLA_HEADS):
        b = h * HEAD_PAD
        qT_ref[b:b + MLA_NOPE, :] = (qT[b:b + MLA_NOPE, :] * qscale).astype(BF16)
        t1 = qT[b + MLA_NOPE:b + MLA_NOPE + ROPE_HALF, :]
        t2 = qT[b + MLA_NOPE + ROPE_HALF:b + MLA_QK, :]
        qT_ref[b + MLA_NOPE:b + MLA_NOPE + ROPE_HALF, :] = (t1 * cq - t2 * sq).astype(BF16)
        qT_ref[b + MLA_NOPE + ROPE_HALF:b + MLA_QK, :] = (t1 * sq + t2 * cq).astype(BF16)
        qT_ref[b + MLA_QK:b + HEAD_PAD, :] = jnp.zeros((HEAD_PAD - MLA_QK, qT.shape[1]), BF16)

    kn_ref[...] = _dot(kvn.astype(BF16), wk_ref[...]).astype(BF16)
    vT_ref[...] = _dot(wvT_ref[...], kvnT).astype(BF16)

    dil = _dot_nt(xb, win_ref[C_QD:C_GATE, :])
    dscale = (DIL_HEAD_DIM ** -0.5) * LOG2E
    qdT_ref[...] = (dil[:, :DIL_WIDTH] * dscale).T.astype(BF16)
    kd_ref[...] = dil[:, DIL_WIDTH:2 * DIL_WIDTH].astype(BF16)
    vdT_ref[...] = dil[:, 2 * DIL_WIDTH:].T.astype(BF16)

    gates = _dot_nt(xb, win_ref[C_GATE:IN_AUG, :]) + bg_ref[...]
    g_ref[...] = jax.nn.sigmoid(gates).astype(BF16)

    kr = lat[:, C_KR:C_QD]
    kr = kr * ck_ref[...] + pltpu.roll(kr * sk_ref[...], LANES - MLA_ROPE, axis=1)
    kr_ref[...] = kr.astype(BF16)


def _proj_call(x2, win, wqT, wk, wvT, gq, gkv, bg, cq, sq, ck, sk):
    T = x2.shape[0]
    tm = ROW_TILE
    tiles_per_seq = SEQ // tm
    const = lambda i: (0, 0)
    row = lambda i: (i, 0)
    col = lambda i: (0, i)
    pos_col = lambda i: (0, i % tiles_per_seq)
    pos_row = lambda i: (i % tiles_per_seq, 0)

    def wspec(a):
        return pl.BlockSpec(a.shape, const, pipeline_mode=pl.Buffered(1))

    out_shapes = (
        jax.ShapeDtypeStruct((MLA_HEADS * HEAD_PAD, T), BF16),
        jax.ShapeDtypeStruct((T, MLA_HEADS * HEAD_PAD), BF16),
        jax.ShapeDtypeStruct((T, LANES), BF16),
        jax.ShapeDtypeStruct((MLA_WIDTH, T), BF16),
        jax.ShapeDtypeStruct((DIL_WIDTH, T), BF16),
        jax.ShapeDtypeStruct((T, DIL_WIDTH), BF16),
        jax.ShapeDtypeStruct((DIL_WIDTH, T), BF16),
        jax.ShapeDtypeStruct((T, N_BRANCH * D_MODEL), BF16),
    )
    out_specs = (
        pl.BlockSpec((MLA_HEADS * HEAD_PAD, tm), col),
        pl.BlockSpec((tm, MLA_HEADS * HEAD_PAD), row),
        pl.BlockSpec((tm, LANES), row),
        pl.BlockSpec((MLA_WIDTH, tm), col),
        pl.BlockSpec((DIL_WIDTH, tm), col),
        pl.BlockSpec((tm, DIL_WIDTH), row),
        pl.BlockSpec((DIL_WIDTH, tm), col),
        pl.BlockSpec((tm, N_BRANCH * D_MODEL), row),
    )
    in_specs = [
        pl.BlockSpec((tm, D_MODEL), row),
        wspec(win), wspec(wqT), wspec(wk), wspec(wvT), wspec(gq), wspec(gkv), wspec(bg),
        pl.BlockSpec((ROPE_HALF, tm), pos_col),
        pl.BlockSpec((ROPE_HALF, tm), pos_col),
        pl.BlockSpec((tm, LANES), pos_row),
        pl.BlockSpec((tm, LANES), pos_row),
    ]
    return pl.pallas_call(
        _proj_kernel,
        grid=(T // tm,),
        in_specs=in_specs,
        out_specs=out_specs,
        out_shape=out_shapes,
        compiler_params=pltpu.CompilerParams(
            dimension_semantics=("arbitrary",), vmem_limit_bytes=VMEM_LIMIT),
        name="proj",
    )(x2, win, wqT, wk, wvT, gq, gkv, bg, cq, sq, ck, sk)


class _Head(NamedTuple):
    qT: object
    keys: object
    valsT: object
    oT: object
    bias_chunk: object


def _causal_attention(heads, s_refs):
    def qcols(i):
        return slice(i * ATT_Q, (i + 1) * ATT_Q)

    def krows(j):
        return slice(j * ATT_K, (j + 1) * ATT_K)

    def first_col(i, j):
        return max(0, j * ATT_K - i * ATT_Q)

    def widen(x, lo, fill):
        if lo == 0:
            return x
        return jnp.concatenate([jnp.full((x.shape[0], lo), fill, x.dtype), x], axis=1)

    def s_of(item):
        g, i = item
        return s_refs[(g % HEAD_GROUP) * N_Q_TILES + i]

    def scores(item, j, q_tile):
        hd, i = heads[item[0]], item[1]
        lo = first_col(i, j)
        sT = _dot(hd.keys[krows(j), :], q_tile[:, lo:])
        b = hd.bias_chunk(i, j)
        if b is not None:
            sT = sT + b[:, lo:]
        s_of(item)[krows(j), lo:] = sT
        return widen(jnp.max(sT, axis=0, keepdims=True), lo, NEG)

    def weights(item, j, m):
        hd, i = heads[item[0]], item[1]
        lo = first_col(i, j)
        p = jnp.exp2(s_of(item)[krows(j), lo:] - m[:, lo:])
        return widen(_dot(hd.valsT[:, krows(j)], p.astype(BF16)), lo, 0.0)

    def fold(op, old, new):
        return new if old is None else op(old, new)

    def n_chunks(item):
        return _n_chunks(item[1]) if item is not None else 0

    def q_of(item):
        return heads[item[0]].qT[:, qcols(item[1])] if item is not None else None

    items = [(g, i) for base in range(0, len(heads), HEAD_GROUP)
             for i in reversed(range(N_Q_TILES)) for g in range(base, base + HEAD_GROUP)]
    m = []
    for item in items[:SCORE_LEAD]:
        q_tile, mi = q_of(item), None
        for j in range(n_chunks(item)):
            mi = fold(jnp.maximum, mi, scores(item, j, q_tile))
        m.append(mi)
    for pos, item in enumerate(items):
        ahead = items[pos + SCORE_LEAD] if pos + SCORE_LEAD < len(items) else None
        q_ahead, m_ahead, acc = q_of(ahead), None, None
        for j in range(max(n_chunks(item), n_chunks(ahead))):
            if j < n_chunks(item):
                acc = fold(jnp.add, acc, weights(item, j, m[pos]))
            if j < n_chunks(ahead):
                m_ahead = fold(jnp.maximum, m_ahead, scores(ahead, j, q_ahead))
        m.append(m_ahead)
        hd, i = heads[item[0]], item[1]
        d_v = hd.oT.shape[0]
        hd.oT[:, qcols(i)] = (acc[:d_v, :] / acc[d_v:d_v + 1, :]).astype(BF16)


def _n_chunks(i):
    return (i + 1) * (ATT_Q // ATT_K)


def _causal_bias(chunk_offset):
    key = lax.broadcasted_iota(jnp.int32, (ATT_K, ATT_Q), 0) + chunk_offset
    qry = lax.broadcasted_iota(jnp.int32, (ATT_K, ATT_Q), 1)
    return jnp.where(key > qry, NEG, 0.0).astype(F32)


ONES_ROWS = 16


def _values_with_ones(vT, va_ref):
    d_v = vT.shape[0]
    va_ref[0:d_v, :] = vT
    va_ref[d_v:, :] = jnp.ones((ONES_ROWS, va_ref.shape[1]), BF16)


def _att_scratch(d_v):
    return [pltpu.VMEM((HEADS_PER_STEP, d_v + ONES_ROWS, SEQ), BF16)] + [
        pltpu.VMEM((_n_chunks(i) * ATT_K, ATT_Q), F32)
        for _ in range(HEAD_GROUP) for i in range(N_Q_TILES)]


def _mla_kernel(qT_ref, kn_ref, kr_ref, vT_ref, oT_ref, kf_ref, va_ref, *s_refs):
    per_tile = ATT_Q // ATT_K
    masks = [_causal_bias(c * ATT_K) for c in range(per_tile)]

    def bias_chunk(i, j):
        c = j - i * per_tile
        return masks[c] if c >= 0 else None

    heads = []
    for g in range(HEADS_PER_STEP):
        qk = pl.ds(g * HEAD_PAD, HEAD_PAD)
        vo = pl.ds(g * MLA_V, MLA_V)
        kf_ref[:, qk] = kn_ref[:, qk] + kr_ref[...]
        _values_with_ones(vT_ref[vo, :], va_ref.at[g])
        heads.append(_Head(qT_ref.at[qk, :], kf_ref.at[:, qk], va_ref.at[g],
                           oT_ref.at[vo, :], bias_chunk))
    _causal_attention(heads, s_refs)


def _mla_call(qT, kn, kr, vT, batch):
    T = kn.shape[0]
    g = HEADS_PER_STEP
    return pl.pallas_call(
        _mla_kernel,
        grid=(batch, MLA_HEADS // g),
        in_specs=[
            pl.BlockSpec((g * HEAD_PAD, SEQ), lambda b, h: (h, b)),
            pl.BlockSpec((SEQ, g * HEAD_PAD), lambda b, h: (b, h)),
            pl.BlockSpec((SEQ, LANES), lambda b, h: (b, 0)),
            pl.BlockSpec((g * MLA_V, SEQ), lambda b, h: (h, b)),
        ],
        out_specs=pl.BlockSpec((g * MLA_V, SEQ), lambda b, h: (h, b)),
        out_shape=jax.ShapeDtypeStruct((MLA_WIDTH, T), BF16),
        scratch_shapes=[pltpu.VMEM((SEQ, g * HEAD_PAD), BF16)] + _att_scratch(MLA_V),
        compiler_params=pltpu.CompilerParams(
            dimension_semantics=("arbitrary", "arbitrary"), vmem_limit_bytes=VMEM_LIMIT),
        name="mla_attn",
    )(qT, kn, kr, vT)


N_DIL_OFFSETS = SEQ // ATT_K + 1


def _dil_kernel(slopes_ref, qT_ref, k_ref, vT_ref, dist_ref, logc_ref, oT_ref,
                q2_ref, bias_ref, va_ref, *s_refs):
    per_tile = ATT_Q // ATT_K
    d = DIL_HEAD_DIM

    @pl.when(pl.program_id(1) == 0)
    def _():
        for g in range(HEADS_PER_STEP):
            slope = slopes_ref[pl.program_id(0) * HEADS_PER_STEP + g]
            bias_ref[g] = logc_ref[...] - slope * dist_ref[...]

    heads = []
    for g in range(HEADS_PER_STEP):
        rows = pl.ds(g * d, d)
        pair = pl.ds((g * d // LANES) * LANES, LANES)
        q2_ref[g] = jnp.zeros(q2_ref.shape[1:], BF16)
        q2_ref[g, pl.ds(g * d % LANES, d), :] = qT_ref[rows, :]
        _values_with_ones(vT_ref[rows, :], va_ref.at[g])

        def bias_chunk(i, j, g=g):
            return jnp.concatenate(
                [bias_ref[g, i * per_tile + c - j + 1] for c in range(per_tile)], axis=1)

        heads.append(_Head(q2_ref.at[g], k_ref.at[:, pair], va_ref.at[g],
                           oT_ref.at[rows, :], bias_chunk))
    _causal_attention(heads, s_refs)


def _dil_call(qdT, kd, vdT, batch):
    T = kd.shape[0]
    g = HEADS_PER_STEP
    width = g * DIL_HEAD_DIM
    assert width % LANES == 0
    slopes = jnp.asarray(
        [LOG2E * 2.0 ** (-8.0 * (i + 1) / DIL_HEADS) for i in range(DIL_HEADS)], F32)
    dist, logc = _dilated_tables()
    table = pl.BlockSpec(dist.shape, lambda h, b: (0, 0, 0), pipeline_mode=pl.Buffered(1))
    return pl.pallas_call(
        _dil_kernel,
        grid=(DIL_HEADS // g, batch),
        in_specs=[
            pl.BlockSpec(memory_space=pltpu.SMEM),
            pl.BlockSpec((width, SEQ), lambda h, b: (h, b)),
            pl.BlockSpec((SEQ, width), lambda h, b: (b, h)),
            pl.BlockSpec((width, SEQ), lambda h, b: (h, b)),
            table, table,
        ],
        out_specs=pl.BlockSpec((width, SEQ), lambda h, b: (h, b)),
        out_shape=jax.ShapeDtypeStruct((DIL_WIDTH, T), BF16),
        scratch_shapes=[pltpu.VMEM((g, LANES, SEQ), BF16),
                        pltpu.VMEM((g, N_DIL_OFFSETS, ATT_K, ATT_K), F32)]
        + _att_scratch(DIL_HEAD_DIM),
        compiler_params=pltpu.CompilerParams(
            dimension_semantics=("arbitrary", "arbitrary"), vmem_limit_bytes=VMEM_LIMIT),
        name="dil_attn",
    )(slopes, qdT, kd, vdT, dist, logc)


def _dilated_tables():
    off = np.arange(-1, N_DIL_OFFSETS - 1)[:, None, None]
    key = np.arange(ATT_K)[None, :, None]
    qry = np.arange(ATT_K)[None, None, :]
    dist = ATT_K * off + qry - key
    count = np.zeros(dist.shape, np.int32)
    for window, dilation in DIL_PATTERNS:
        count += ((dist >= 0) & (dist % dilation == 0) & (dist <= window)).astype(np.int32)
    logc = np.where(count > 0, np.log2(np.maximum(count, 1)), NEG)
    return jnp.asarray(dist, F32), jnp.asarray(logc, F32)


def _post_kernel(alpha, x_ref, oaT_ref, obT_ref, g_ref, woa_ref, wob_ref, wout_ref,
                 l1g_ref, l1b_ref, w1_ref, w2_ref, l2g_ref, l2b_ref, out_ref):
    n = x_ref.shape[0] // POST_ROW_GROUPS
    groups = [pl.ds(r * n, n) for r in range(POST_ROW_GROUPS)]
    ya = [_dot(oaT_ref[:, rows].T, woa_ref[...]) for rows in groups]
    yb = [_dot(obT_ref[:, rows].T, wob_ref[...]) for rows in groups]
    mixed = []
    for r, rows in enumerate(groups):
        g = g_ref[rows, :].astype(F32)
        mix = (g[:, :D_MODEL] * ya[r] + g[:, D_MODEL:] * yb[r]).astype(BF16)
        mixed.append(_dot(mix, wout_ref[...]))
    h = [_layer_norm(alpha * x_ref[rows, :] + mixed[r], l1g_ref[...], l1b_ref[...])
         for r, rows in enumerate(groups)]
    hb = [t.astype(BF16) for t in h]
    f = [None] * POST_ROW_GROUPS
    for c in range(D_FF // FF_CHUNK):
        cols = slice(c * FF_CHUNK, (c + 1) * FF_CHUNK)
        u = [jnp.maximum(_dot(t, w1_ref[:, cols]), 0.0) for t in hb]
        for r in range(POST_ROW_GROUPS):
            fc = _dot((u[r] * u[r]).astype(BF16), w2_ref[cols, :])
            f[r] = fc if f[r] is None else f[r] + fc
    for r, rows in enumerate(groups):
        out_ref[rows, :] = _layer_norm(alpha * h[r] + f[r], l2g_ref[...], l2b_ref[...])


def _post_call(alpha, x2, oaT, obT, g, woa, wob, wout, l1g, l1b, w1, w2, l2g, l2b):
    T = x2.shape[0]
    tm = ROW_TILE
    const = lambda i: (0, 0)
    row = lambda i: (i, 0)
    col = lambda i: (0, i)

    def wspec(a):
        return pl.BlockSpec(a.shape, const, pipeline_mode=pl.Buffered(1))

    return pl.pallas_call(
        functools.partial(_post_kernel, alpha),
        grid=(T // tm,),
        in_specs=[
            pl.BlockSpec((tm, D_MODEL), row),
            pl.BlockSpec((MLA_WIDTH, tm), col),
            pl.BlockSpec((DIL_WIDTH, tm), col),
            pl.BlockSpec((tm, N_BRANCH * D_MODEL), row),
            wspec(woa), wspec(wob), wspec(wout), wspec(l1g), wspec(l1b),
            wspec(w1), wspec(w2), wspec(l2g), wspec(l2b),
        ],
        out_specs=pl.BlockSpec((tm, D_MODEL), row),
        out_shape=jax.ShapeDtypeStruct((T, D_MODEL), F32),
        compiler_params=pltpu.CompilerParams(
            dimension_semantics=("arbitrary",), vmem_limit_bytes=VMEM_LIMIT),
        name="post",
    )(x2, oaT, obT, g, woa, wob, wout, l1g, l1b, w1, w2, l2g, l2b)


def _rotate_half_rows(w):
    half = w.shape[0] // 2
    return jnp.concatenate([-w[half:], w[:half]], axis=0)


def _rope_tables():
    inv = jnp.power(ROPE_THETA, -jnp.arange(ROPE_HALF, dtype=F32) / ROPE_HALF)
    ang = jnp.arange(SEQ).astype(F32)[:, None] * inv[None, :]
    cos, sin = jnp.cos(ang), jnp.sin(ang)
    qscale = (MLA_QK ** -0.5) * LOG2E
    cq = (cos * qscale).T
    sq = (sin * qscale).T
    zeros = jnp.zeros((SEQ, MLA_NOPE), F32)
    z32 = jnp.zeros((SEQ, MLA_ROPE), F32)
    ck = jnp.concatenate([zeros, cos, cos, z32], axis=1)
    sk = jnp.concatenate([zeros, z32, sin, sin], axis=1)
    return cq, sq, ck, sk


def _layer(x2, batch, alpha, w_in, b_gate, g_q_a, w_uq, g_kv_a, w_ukv, w_o_mla, w_o_dil,
           w_out, ln1_g, ln1_b, w_ff1, w_ff2, ln2_g, ln2_b):
    s0, s1, s2, s3 = Q_LORA, Q_LORA + KV_LORA, Q_LORA + KV_LORA + MLA_ROPE, \
        Q_LORA + KV_LORA + MLA_ROPE + 3 * DIL_WIDTH
    w_inT = w_in.T.astype(BF16)
    w_kr = w_inT[s1:s2]
    win = jnp.concatenate(
        [w_inT[:s1], jnp.zeros((MLA_NOPE, D_MODEL), BF16), w_kr, _rotate_half_rows(w_kr),
         w_inT[s2:]], axis=0)

    wq = w_uq.reshape(Q_LORA, MLA_HEADS, MLA_QK)
    wq = jnp.pad(wq, ((0, 0), (0, 0), (0, HEAD_PAD - MLA_QK)))
    wqT = wq.reshape(Q_LORA, MLA_HEADS * HEAD_PAD).T.astype(BF16)
    wkv = w_ukv.reshape(KV_LORA, MLA_HEADS, MLA_NOPE + MLA_V)
    wk = jnp.pad(wkv[:, :, :MLA_NOPE], ((0, 0), (0, 0), (0, HEAD_PAD - MLA_NOPE)))
    wk = wk.reshape(KV_LORA, MLA_HEADS * HEAD_PAD).astype(BF16)
    wvT = wkv[:, :, MLA_NOPE:].reshape(KV_LORA, MLA_WIDTH).T.astype(BF16)

    cq, sq, ck, sk = _rope_tables()
    qT, kn, kr, vT, qdT, kd, vdT, g = _proj_call(
        x2, win, wqT, wk, wvT, g_q_a[None], g_kv_a[None],
        b_gate.reshape(1, N_BRANCH * D_MODEL), cq, sq, ck, sk)

    oaT = _mla_call(qT, kn, kr, vT, batch)
    obT = _dil_call(qdT, kd, vdT, batch)

    return _post_call(alpha, x2, oaT, obT, g, w_o_mla.astype(BF16), w_o_dil.astype(BF16),
                      w_out.astype(BF16), ln1_g[None], ln1_b[None], w_ff1.astype(BF16),
                      w_ff2.astype(BF16), ln2_g[None], ln2_b[None])


def kernel(x, w_in, b_gate, g_q_a, w_uq, g_kv_a, w_ukv, w_o_mla, w_o_dil, w_out,
           ln1_g, ln1_b, w_ff1, w_ff2, ln2_g, ln2_b):
    batch, seq, d = x.shape
    assert (seq, d) == (SEQ, D_MODEL)
    depth = w_in.shape[0]
    alpha = (2 * depth) ** 0.25
    x2 = x.reshape(batch * seq, d)
    for l in range(depth):
        x2 = _layer(x2, batch, alpha, w_in[l], b_gate[l], g_q_a[l], w_uq[l], g_kv_a[l],
                    w_ukv[l], w_o_mla[l], w_o_dil[l], w_out[l], ln1_g[l], ln1_b[l],
                    w_ff1[l], w_ff2[l], ln2_g[l], ln2_b[l])
    return x2.reshape(batch, seq, d)
```

```python
import functools
import math
from typing import NamedTuple

import jax
import jax.numpy as jnp
import numpy as np
from jax import lax
from jax.experimental import pallas as pl
from jax.experimental.pallas import tpu as pltpu

D_MODEL = 1024
SEQ = 2048

MLA_HEADS = 8
MLA_NOPE = 64
MLA_ROPE = 32
MLA_V = 64
Q_LORA = 384
KV_LORA = 256
ROPE_THETA = 10000.0
MLA_QK = MLA_NOPE + MLA_ROPE
MLA_WIDTH = MLA_HEADS * MLA_V
ROPE_HALF = MLA_ROPE // 2

DIL_HEADS = 8
DIL_HEAD_DIM = 64
DIL_PATTERNS = ((128, 1), (512, 4), (2048, 16))
DIL_WIDTH = DIL_HEADS * DIL_HEAD_DIM

N_BRANCH = 2
D_FF = 4 * D_MODEL
LN_EPS = 1e-5
RMS_EPS = 1e-6
NEG = -1e30

LOG2E = math.log2(math.e)

LANES = 128
MXU_DIM = 256
VMEM_LIMIT = 56 * 1024 * 1024

HEAD_PAD = LANES
ROW_TILE = 512
ATT_Q = 2 * MXU_DIM
ATT_K = MXU_DIM
N_Q_TILES = SEQ // ATT_Q
HEADS_PER_STEP = 4
HEAD_GROUP = 2
SCORE_LEAD = 2
FF_CHUNK = 1024
POST_ROW_GROUPS = 2

C_QA = 0
C_KVA = C_QA + Q_LORA
C_KR = C_KVA + KV_LORA
C_QD = C_KR + LANES
C_KD = C_QD + DIL_WIDTH
C_VD = C_KD + DIL_WIDTH
C_GATE = C_VD + DIL_WIDTH
IN_AUG = C_GATE + N_BRANCH * D_MODEL

BF16 = jnp.bfloat16
F32 = jnp.float32


def _dot(a, b):
    return jnp.dot(a, b, preferred_element_type=F32)


def _dot_nt(a, b):
    return lax.dot_general(a, b, (((1,), (1,)), ((), ())), preferred_element_type=F32)


def _rms(t, g):
    return t * lax.rsqrt(jnp.mean(t * t, axis=-1, keepdims=True) + RMS_EPS) * g


def _layer_norm(t, g, b):
    mu = jnp.mean(t, axis=-1, keepdims=True)
    d = t - mu
    var = jnp.mean(d * d, axis=-1, keepdims=True)
    return d * lax.rsqrt(var + LN_EPS) * g + b


def _proj_kernel(x_ref, win_ref, wqT_ref, wk_ref, wvT_ref, gq_ref, gkv_ref, bg_ref,
                 cq_ref, sq_ref, ck_ref, sk_ref,
                 qT_ref, k_ref, vT_ref, qdT_ref, kd_ref, vdT_ref, g_ref):
    xb = x_ref[...].astype(BF16)

    lat = _dot_nt(xb, win_ref[C_QA:C_QD, :])
    qn = _rms(lat[:, C_QA:C_KVA], gq_ref[...])
    kvn = _rms(lat[:, C_KVA:C_KR], gkv_ref[...])
    qnT = qn.T.astype(BF16)
    kvnT = kvn.T.astype(BF16)

    qT = _dot(wqT_ref[...], qnT)
    cq = cq_ref[...]
    sq = sq_ref[...]
    qscale = (MLA_QK ** -0.5) * LOG2E
    for h in range(MLA_HEADS):
        b = h * HEAD_PAD
        qT_ref[b:b + MLA_NOPE, :] = (qT[b:b + MLA_NOPE, :] * qscale).astype(BF16)
        t1 = qT[b + MLA_NOPE:b + MLA_NOPE + ROPE_HALF, :]
        t2 = qT[b + MLA_NOPE + ROPE_HALF:b + MLA_QK, :]
        qT_ref[b + MLA_NOPE:b + MLA_NOPE + ROPE_HALF, :] = (t1 * cq - t2 * sq).astype(BF16)
        qT_ref[b + MLA_NOPE + ROPE_HALF:b + MLA_QK, :] = (t1 * sq + t2 * cq).astype(BF16)
        qT_ref[b + MLA_QK:b + HEAD_PAD, :] = jnp.zeros((HEAD_PAD - MLA_QK, qT.shape[1]), BF16)

    kr = lat[:, C_KR:C_QD]
    kr = kr * ck_ref[...] + pltpu.roll(kr * sk_ref[...], LANES - MLA_ROPE, axis=1)
    k = _dot(kvn.astype(BF16), wk_ref[...]) + jnp.concatenate([kr] * MLA_HEADS, axis=1)
    k_ref[...] = k.astype(BF16)
    vT_ref[...] = _dot(wvT_ref[...], kvnT).astype(BF16)

    dil = _dot_nt(xb, win_ref[C_QD:C_GATE, :])
    dscale = (DIL_HEAD_DIM ** -0.5) * LOG2E
    qdT_ref[...] = (dil[:, :DIL_WIDTH] * dscale).T.astype(BF16)
    kd_ref[...] = dil[:, DIL_WIDTH:2 * DIL_WIDTH].astype(BF16)
    vdT_ref[...] = dil[:, 2 * DIL_WIDTH:].T.astype(BF16)

    gates = _dot_nt(xb, win_ref[C_GATE:IN_AUG, :]) + bg_ref[...]
    g_ref[...] = jax.nn.sigmoid(gates).astype(BF16)


def _proj_call(x2, win, wqT, wk, wvT, gq, gkv, bg, cq, sq, ck, sk):
    T = x2.shape[0]
    tm = ROW_TILE
    tiles_per_seq = SEQ // tm
    const = lambda i: (0, 0)
    row = lambda i: (i, 0)
    col = lambda i: (0, i)
    pos_col = lambda i: (0, i % tiles_per_seq)
    pos_row = lambda i: (i % tiles_per_seq, 0)

    def wspec(a):
        return pl.BlockSpec(a.shape, const, pipeline_mode=pl.Buffered(1))

    out_shapes = (
        jax.ShapeDtypeStruct((MLA_HEADS * HEAD_PAD, T), BF16),
        jax.ShapeDtypeStruct((T, MLA_HEADS * HEAD_PAD), BF16),
        jax.ShapeDtypeStruct((MLA_WIDTH, T), BF16),
        jax.ShapeDtypeStruct((DIL_WIDTH, T), BF16),
        jax.ShapeDtypeStruct((T, DIL_WIDTH), BF16),
        jax.ShapeDtypeStruct((DIL_WIDTH, T), BF16),
        jax.ShapeDtypeStruct((T, N_BRANCH * D_MODEL), BF16),
    )
    out_specs = (
        pl.BlockSpec((MLA_HEADS * HEAD_PAD, tm), col),
        pl.BlockSpec((tm, MLA_HEADS * HEAD_PAD), row),
        pl.BlockSpec((MLA_WIDTH, tm), col),
        pl.BlockSpec((DIL_WIDTH, tm), col),
        pl.BlockSpec((tm, DIL_WIDTH), row),
        pl.BlockSpec((DIL_WIDTH, tm), col),
        pl.BlockSpec((tm, N_BRANCH * D_MODEL), row),
    )
    in_specs = [
        pl.BlockSpec((tm, D_MODEL), row),
        wspec(win), wspec(wqT), wspec(wk), wspec(wvT), wspec(gq), wspec(gkv), wspec(bg),
        pl.BlockSpec((ROPE_HALF, tm), pos_col),
        pl.BlockSpec((ROPE_HALF, tm), pos_col),
        pl.BlockSpec((tm, LANES), pos_row),
        pl.BlockSpec((tm, LANES), pos_row),
    ]
    return pl.pallas_call(
        _proj_kernel,
        grid=(T // tm,),
        in_specs=in_specs,
        out_specs=out_specs,
        out_shape=out_shapes,
        compiler_params=pltpu.CompilerParams(
            dimension_semantics=("arbitrary",), vmem_limit_bytes=VMEM_LIMIT),
        name="proj",
    )(x2, win, wqT, wk, wvT, gq, gkv, bg, cq, sq, ck, sk)


class _Head(NamedTuple):
    qT: object
    keys: object
    valsT: object
    oT: object
    bias_chunk: object


def _causal_attention(heads, s_refs, exp_dtype):
    def qcols(i):
        return slice(i * ATT_Q, (i + 1) * ATT_Q)

    def krows(j):
        return slice(j * ATT_K, (j + 1) * ATT_K)

    def first_col(i, j):
        return max(0, j * ATT_K - i * ATT_Q)

    def widen(x, lo, fill):
        if lo == 0:
            return x
        return jnp.concatenate([jnp.full((x.shape[0], lo), fill, x.dtype), x], axis=1)

    def s_of(item):
        g, i = item
        return s_refs[(g % HEAD_GROUP) * N_Q_TILES + i]

    def scores(item, j, q_tile):
        hd, i = heads[item[0]], item[1]
        lo = first_col(i, j)
        sT = _dot(hd.keys[krows(j), :], q_tile[:, lo:])
        b = hd.bias_chunk(i, j)
        if b is not None:
            sT = sT + b[:, lo:]
        s_of(item)[krows(j), lo:] = sT
        return widen(jnp.max(sT, axis=0, keepdims=True), lo, NEG)

    def weights(item, j, m):
        hd, i = heads[item[0]], item[1]
        lo = first_col(i, j)
        p = jnp.exp2((s_of(item)[krows(j), lo:] - m[:, lo:]).astype(exp_dtype))
        return widen(_dot(hd.valsT[:, krows(j)], p.astype(BF16)), lo, 0.0)

    def fold(op, old, new):
        return new if old is None else op(old, new)

    def n_chunks(item):
        return _n_chunks(item[1]) if item is not None else 0

    def q_of(item):
        return heads[item[0]].qT[:, qcols(item[1])] if item is not None else None

    items = [(g, i) for base in range(0, len(heads), HEAD_GROUP)
             for i in reversed(range(N_Q_TILES)) for g in range(base, base + HEAD_GROUP)]
    m = []
    for item in items[:SCORE_LEAD]:
        q_tile, mi = q_of(item), None
        for j in range(n_chunks(item)):
            mi = fold(jnp.maximum, mi, scores(item, j, q_tile))
        m.append(mi)
    for pos, item in enumerate(items):
        ahead = items[pos + SCORE_LEAD] if pos + SCORE_LEAD < len(items) else None
        q_ahead, m_ahead, acc = q_of(ahead), None, None
        for j in range(max(n_chunks(item), n_chunks(ahead))):
            if j < n_chunks(item):
                acc = fold(jnp.add, acc, weights(item, j, m[pos]))
            if j < n_chunks(ahead):
                m_ahead = fold(jnp.maximum, m_ahead, scores(ahead, j, q_ahead))
        m.append(m_ahead)
        hd, i = heads[item[0]], item[1]
        d_v = hd.oT.shape[0]
        hd.oT[:, qcols(i)] = (acc[:d_v, :] / acc[d_v:d_v + 1, :]).astype(BF16)


def _n_chunks(i):
    return (i + 1) * (ATT_Q // ATT_K)


def _causal_bias(chunk_offset):
    key = lax.broadcasted_iota(jnp.int32, (ATT_K, ATT_Q), 0) + chunk_offset
    qry = lax.broadcasted_iota(jnp.int32, (ATT_K, ATT_Q), 1)
    return jnp.where(key > qry, NEG, 0.0).astype(F32)


ONES_ROWS = 16


def _values_with_ones(vT, va_ref):
    d_v = vT.shape[0]
    va_ref[0:d_v, :] = vT
    va_ref[d_v:, :] = jnp.ones((ONES_ROWS, va_ref.shape[1]), BF16)


def _att_scratch(d_v):
    return [pltpu.VMEM((HEADS_PER_STEP, d_v + ONES_ROWS, SEQ), BF16)] + [
        pltpu.VMEM((_n_chunks(i) * ATT_K, ATT_Q), F32)
        for _ in range(HEAD_GROUP) for i in range(N_Q_TILES)]


def _mla_kernel(qT_ref, k_ref, vT_ref, oT_ref, va_ref, *s_refs):
    per_tile = ATT_Q // ATT_K
    masks = [_causal_bias(c * ATT_K) for c in range(per_tile)]

    def bias_chunk(i, j):
        c = j - i * per_tile
        return masks[c] if c >= 0 else None

    heads = []
    for g in range(HEADS_PER_STEP):
        qk = pl.ds(g * HEAD_PAD, HEAD_PAD)
        vo = pl.ds(g * MLA_V, MLA_V)
        _values_with_ones(vT_ref[vo, :], va_ref.at[g])
        heads.append(_Head(qT_ref.at[qk, :], k_ref.at[:, qk], va_ref.at[g],
                           oT_ref.at[vo, :], bias_chunk))
    _causal_attention(heads, s_refs, F32)


def _mla_call(qT, k, vT, batch):
    T = k.shape[0]
    g = HEADS_PER_STEP
    return pl.pallas_call(
        _mla_kernel,
        grid=(batch, MLA_HEADS // g),
        in_specs=[
            pl.BlockSpec((g * HEAD_PAD, SEQ), lambda b, h: (h, b)),
            pl.BlockSpec((SEQ, g * HEAD_PAD), lambda b, h: (b, h)),
            pl.BlockSpec((g * MLA_V, SEQ), lambda b, h: (h, b)),
        ],
        out_specs=pl.BlockSpec((g * MLA_V, SEQ), lambda b, h: (h, b)),
        out_shape=jax.ShapeDtypeStruct((MLA_WIDTH, T), BF16),
        scratch_shapes=_att_scratch(MLA_V),
        compiler_params=pltpu.CompilerParams(
            dimension_semantics=("arbitrary", "arbitrary"), vmem_limit_bytes=VMEM_LIMIT),
        name="mla_attn",
    )(qT, k, vT)


N_DIL_OFFSETS = SEQ // ATT_K + 1


def _dil_kernel(slopes_ref, qT_ref, k_ref, vT_ref, dist_ref, logc_ref, oT_ref,
                q2_ref, bias_ref, va_ref, *s_refs):
    per_tile = ATT_Q // ATT_K
    d = DIL_HEAD_DIM

    @pl.when(pl.program_id(1) == 0)
    def _():
        for g in range(HEADS_PER_STEP):
            slope = slopes_ref[pl.program_id(0) * HEADS_PER_STEP + g]
            bias_ref[g] = logc_ref[...] - slope * dist_ref[...]

    heads = []
    for g in range(HEADS_PER_STEP):
        rows = pl.ds(g * d, d)
        pair = pl.ds((g * d // LANES) * LANES, LANES)
        q2_ref[g] = jnp.zeros(q2_ref.shape[1:], BF16)
        q2_ref[g, pl.ds(g * d % LANES, d), :] = qT_ref[rows, :]
        _values_with_ones(vT_ref[rows, :], va_ref.at[g])

        def bias_chunk(i, j, g=g):
            return jnp.concatenate(
                [bias_ref[g, i * per_tile + c - j + 1] for c in range(per_tile)], axis=1)

        heads.append(_Head(q2_ref.at[g], k_ref.at[:, pair], va_ref.at[g],
                           oT_ref.at[rows, :], bias_chunk))
    _causal_attention(heads, s_refs, BF16)


def _dil_call(qdT, kd, vdT, batch):
    T = kd.shape[0]
    g = HEADS_PER_STEP
    width = g * DIL_HEAD_DIM
    assert width % LANES == 0
    slopes = jnp.asarray(
        [LOG2E * 2.0 ** (-8.0 * (i + 1) / DIL_HEADS) for i in range(DIL_HEADS)], F32)
    dist, logc = _dilated_tables()
    table = pl.BlockSpec(dist.shape, lambda h, b: (0, 0, 0), pipeline_mode=pl.Buffered(1))
    return pl.pallas_call(
        _dil_kernel,
        grid=(DIL_HEADS // g, batch),
        in_specs=[
            pl.BlockSpec(memory_space=pltpu.SMEM),
            pl.BlockSpec((width, SEQ), lambda h, b: (h, b)),
            pl.BlockSpec((SEQ, width), lambda h, b: (b, h)),
            pl.BlockSpec((width, SEQ), lambda h, b: (h, b)),
            table, table,
        ],
        out_specs=pl.BlockSpec((width, SEQ), lambda h, b: (h, b)),
        out_shape=jax.ShapeDtypeStruct((DIL_WIDTH, T), BF16),
        scratch_shapes=[pltpu.VMEM((g, LANES, SEQ), BF16),
                        pltpu.VMEM((g, N_DIL_OFFSETS, ATT_K, ATT_K), F32)]
        + _att_scratch(DIL_HEAD_DIM),
        compiler_params=pltpu.CompilerParams(
            dimension_semantics=("arbitrary", "arbitrary"), vmem_limit_bytes=VMEM_LIMIT),
        name="dil_attn",
    )(slopes, qdT, kd, vdT, dist, logc)


def _dilated_tables():
    off = np.arange(-1, N_DIL_OFFSETS - 1)[:, None, None]
    key = np.arange(ATT_K)[None, :, None]
    qry = np.arange(ATT_K)[None, None, :]
    dist = ATT_K * off + qry - key
    count = np.zeros(dist.shape, np.int32)
    for window, dilation in DIL_PATTERNS:
        count += ((dist >= 0) & (dist % dilation == 0) & (dist <= window)).astype(np.int32)
    logc = np.where(count > 0, np.log2(np.maximum(count, 1)), NEG)
    return jnp.asarray(dist, F32), jnp.asarray(logc, F32)


def _post_kernel(alpha, x_ref, oaT_ref, obT_ref, g_ref, woa_ref, wob_ref, wout_ref,
                 l1g_ref, l1b_ref, w1_ref, w2_ref, l2g_ref, l2b_ref, out_ref):
    n = x_ref.shape[0] // POST_ROW_GROUPS
    groups = [pl.ds(r * n, n) for r in range(POST_ROW_GROUPS)]
    ya = [_dot(oaT_ref[:, rows].T, woa_ref[...]) for rows in groups]
    yb = [_dot(obT_ref[:, rows].T, wob_ref[...]) for rows in groups]
    mixed = []
    for r, rows in enumerate(groups):
        g = g_ref[rows, :].astype(F32)
        mix = (g[:, :D_MODEL] * ya[r] + g[:, D_MODEL:] * yb[r]).astype(BF16)
        mixed.append(_dot(mix, wout_ref[...]))
    h = [_layer_norm(alpha * x_ref[rows, :] + mixed[r], l1g_ref[...], l1b_ref[...])
         for r, rows in enumerate(groups)]
    hb = [t.astype(BF16) for t in h]
    f = [None] * POST_ROW_GROUPS
    for c in range(D_FF // FF_CHUNK):
        cols = slice(c * FF_CHUNK, (c + 1) * FF_CHUNK)
        u = [jnp.maximum(_dot(t, w1_ref[:, cols]), 0.0) for t in hb]
        for r in range(POST_ROW_GROUPS):
            fc = _dot((u[r] * u[r]).astype(BF16), w2_ref[cols, :])
            f[r] = fc if f[r] is None else f[r] + fc
    for r, rows in enumerate(groups):
        out_ref[rows, :] = _layer_norm(alpha * h[r] + f[r], l2g_ref[...], l2b_ref[...])


def _post_call(alpha, x2, oaT, obT, g, woa, wob, wout, l1g, l1b, w1, w2, l2g, l2b):
    T = x2.shape[0]
    tm = ROW_TILE
    const = lambda i: (0, 0)
    row = lambda i: (i, 0)
    col = lambda i: (0, i)

    def wspec(a):
        return pl.BlockSpec(a.shape, const, pipeline_mode=pl.Buffered(1))

    return pl.pallas_call(
        functools.partial(_post_kernel, alpha),
        grid=(T // tm,),
        in_specs=[
            pl.BlockSpec((tm, D_MODEL), row),
            pl.BlockSpec((MLA_WIDTH, tm), col),
            pl.BlockSpec((DIL_WIDTH, tm), col),
            pl.BlockSpec((tm, N_BRANCH * D_MODEL), row),
            wspec(woa), wspec(wob), wspec(wout), wspec(l1g), wspec(l1b),
            wspec(w1), wspec(w2), wspec(l2g), wspec(l2b),
        ],
        out_specs=pl.BlockSpec((tm, D_MODEL), row),
        out_shape=jax.ShapeDtypeStruct((T, D_MODEL), F32),
        compiler_params=pltpu.CompilerParams(
            dimension_semantics=("arbitrary",), vmem_limit_bytes=VMEM_LIMIT),
        name="post",
    )(x2, oaT, obT, g, woa, wob, wout, l1g, l1b, w1, w2, l2g, l2b)


def _rotate_half_rows(w):
    half = w.shape[0] // 2
    return jnp.concatenate([-w[half:], w[:half]], axis=0)


def _rope_tables():
    inv = jnp.power(ROPE_THETA, -jnp.arange(ROPE_HALF, dtype=F32) / ROPE_HALF)
    ang = jnp.arange(SEQ).astype(F32)[:, None] * inv[None, :]
    cos, sin = jnp.cos(ang), jnp.sin(ang)
    qscale = (MLA_QK ** -0.5) * LOG2E
    cq = (cos * qscale).T
    sq = (sin * qscale).T
    zeros = jnp.zeros((SEQ, MLA_NOPE), F32)
    z32 = jnp.zeros((SEQ, MLA_ROPE), F32)
    ck = jnp.concatenate([zeros, cos, cos, z32], axis=1)
    sk = jnp.concatenate([zeros, z32, sin, sin], axis=1)
    return cq, sq, ck, sk


def _layer(x2, batch, alpha, w_in, b_gate, g_q_a, w_uq, g_kv_a, w_ukv, w_o_mla, w_o_dil,
           w_out, ln1_g, ln1_b, w_ff1, w_ff2, ln2_g, ln2_b):
    s0, s1, s2, s3 = Q_LORA, Q_LORA + KV_LORA, Q_LORA + KV_LORA + MLA_ROPE, \
        Q_LORA + KV_LORA + MLA_ROPE + 3 * DIL_WIDTH
    w_inT = w_in.T.astype(BF16)
    w_kr = w_inT[s1:s2]
    win = jnp.concatenate(
        [w_inT[:s1], jnp.zeros((MLA_NOPE, D_MODEL), BF16), w_kr, _rotate_half_rows(w_kr),
         w_inT[s2:]], axis=0)

    wq = w_uq.reshape(Q_LORA, MLA_HEADS, MLA_QK)
    wq = jnp.pad(wq, ((0, 0), (0, 0), (0, HEAD_PAD - MLA_QK)))
    wqT = wq.reshape(Q_LORA, MLA_HEADS * HEAD_PAD).T.astype(BF16)
    wkv = w_ukv.reshape(KV_LORA, MLA_HEADS, MLA_NOPE + MLA_V)
    wk = jnp.pad(wkv[:, :, :MLA_NOPE], ((0, 0), (0, 0), (0, HEAD_PAD - MLA_NOPE)))
    wk = wk.reshape(KV_LORA, MLA_HEADS * HEAD_PAD).astype(BF16)
    wvT = wkv[:, :, MLA_NOPE:].reshape(KV_LORA, MLA_WIDTH).T.astype(BF16)

    cq, sq, ck, sk = _rope_tables()
    qT, k, vT, qdT, kd, vdT, g = _proj_call(
        x2, win, wqT, wk, wvT, g_q_a[None], g_kv_a[None],
        b_gate.reshape(1, N_BRANCH * D_MODEL), cq, sq, ck, sk)

    oaT = _mla_call(qT, k, vT, batch)
    obT = _dil_call(qdT, kd, vdT, batch)

    return _post_call(alpha, x2, oaT, obT, g, w_o_mla.astype(BF16), w_o_dil.astype(BF16),
                      w_out.astype(BF16), ln1_g[None], ln1_b[None], w_ff1.astype(BF16),
                      w_ff2.astype(BF16), ln2_g[None], ln2_b[None])


def kernel(x, w_in, b_gate, g_q_a, w_uq, g_kv_a, w_ukv, w_o_mla, w_o_dil, w_out,
           ln1_g, ln1_b, w_ff1, w_ff2, ln2_g, ln2_b):
    batch, seq, d = x.shape
    assert (seq, d) == (SEQ, D_MODEL)
    depth = w_in.shape[0]
    alpha = (2 * depth) ** 0.25
    x2 = x.reshape(batch * seq, d)
    for l in range(depth):
        x2 = _layer(x2, batch, alpha, w_in[l], b_gate[l], g_q_a[l], w_uq[l], g_kv_a[l],
                    w_ukv[l], w_o_mla[l], w_o_dil[l], w_out[l], ln1_g[l], ln1_b[l],
                    w_ff1[l], w_ff2[l], ln2_g[l], ln2_b[l])
    return x2.reshape(batch, seq, d)
```

```python
import functools
import math
from typing import NamedTuple

import jax
import jax.numpy as jnp
import numpy as np
from jax import lax
from jax.experimental import pallas as pl
from jax.experimental.pallas import tpu as pltpu

D_MODEL = 1024
SEQ = 2048

MLA_HEADS = 8
MLA_NOPE = 64
MLA_ROPE = 32
MLA_V = 64
Q_LORA = 384
KV_LORA = 256
ROPE_THETA = 10000.0
MLA_QK = MLA_NOPE + MLA_ROPE
MLA_WIDTH = MLA_HEADS * MLA_V
ROPE_HALF = MLA_ROPE // 2

DIL_HEADS = 8
DIL_HEAD_DIM = 64
DIL_PATTERNS = ((128, 1), (512, 4), (2048, 16))
DIL_WIDTH = DIL_HEADS * DIL_HEAD_DIM

N_BRANCH = 2
D_FF = 4 * D_MODEL
LN_EPS = 1e-5
RMS_EPS = 1e-6
NEG = -1e30

LOG2E = math.log2(math.e)

LANES = 128
MXU_DIM = 256
VMEM_LIMIT = 56 * 1024 * 1024

HEAD_PAD = LANES
ROW_TILE = 512
ATT_Q = 2 * MXU_DIM
ATT_K = MXU_DIM
N_Q_TILES = SEQ // ATT_Q
HEADS_PER_STEP = 4
HEAD_GROUP = 2
SCORE_SETS = 2
SCORE_LEAD = 2
FF_CHUNK = 1024
POST_ROW_GROUPS = 2

C_QA = 0
C_KVA = C_QA + Q_LORA
C_KR = C_KVA + KV_LORA
C_QD = C_KR + LANES
C_KD = C_QD + DIL_WIDTH
C_VD = C_KD + DIL_WIDTH
C_GATE = C_VD + DIL_WIDTH
IN_AUG = C_GATE + N_BRANCH * D_MODEL

BF16 = jnp.bfloat16
F32 = jnp.float32


def _dot(a, b):
    return jnp.dot(a, b, preferred_element_type=F32)


def _dot_nt(a, b):
    return lax.dot_general(a, b, (((1,), (1,)), ((), ())), preferred_element_type=F32)


def _rms(t, g):
    return t * lax.rsqrt(jnp.mean(t * t, axis=-1, keepdims=True) + RMS_EPS) * g


def _layer_norm(t, g, b):
    mu = jnp.mean(t, axis=-1, keepdims=True)
    d = t - mu
    var = jnp.mean(d * d, axis=-1, keepdims=True)
    return d * lax.rsqrt(var + LN_EPS) * g + b


def _proj_kernel(x_ref, win_ref, wqT_ref, wk_ref, wvT_ref, gq_ref, gkv_ref, bg_ref,
                 cq_ref, sq_ref, ck_ref, sk_ref,
                 qT_ref, k_ref, vT_ref, qdT_ref, kd_ref, vdT_ref, g_ref):
    xb = x_ref[...].astype(BF16)

    lat = _dot_nt(xb, win_ref[C_QA:C_QD, :])
    qn = _rms(lat[:, C_QA:C_KVA], gq_ref[...])
    kvn = _rms(lat[:, C_KVA:C_KR], gkv_ref[...])
    qnT = qn.T.astype(BF16)
    kvnT = kvn.T.astype(BF16)

    qT = _dot(wqT_ref[...], qnT)
    cq = cq_ref[...]
    sq = sq_ref[...]
    qscale = (MLA_QK ** -0.5) * LOG2E
    for h in range(MLA_HEADS):
        b = h * HEAD_PAD
        qT_ref[b:b + MLA_NOPE, :] = (qT[b:b + MLA_NOPE, :] * qscale).astype(BF16)
        t1 = qT[b + MLA_NOPE:b + MLA_NOPE + ROPE_HALF, :]
        t2 = qT[b + MLA_NOPE + ROPE_HALF:b + MLA_QK, :]
        qT_ref[b + MLA_NOPE:b + MLA_NOPE + ROPE_HALF, :] = (t1 * cq - t2 * sq).astype(BF16)
        qT_ref[b + MLA_NOPE + ROPE_HALF:b + MLA_QK, :] = (t1 * sq + t2 * cq).astype(BF16)
        qT_ref[b + MLA_QK:b + HEAD_PAD, :] = jnp.zeros((HEAD_PAD - MLA_QK, qT.shape[1]), BF16)

    kr = lat[:, C_KR:C_QD]
    kr = kr * ck_ref[...] + pltpu.roll(kr * sk_ref[...], LANES - MLA_ROPE, axis=1)
    k = _dot(kvn.astype(BF16), wk_ref[...]) + jnp.concatenate([kr] * MLA_HEADS, axis=1)
    k_ref[...] = k.astype(BF16)
    vT_ref[...] = _dot(wvT_ref[...], kvnT).astype(BF16)

    dil = _dot_nt(xb, win_ref[C_QD:C_GATE, :])
    dscale = (DIL_HEAD_DIM ** -0.5) * LOG2E
    qdT_ref[...] = (dil[:, :DIL_WIDTH] * dscale).T.astype(BF16)
    kd_ref[...] = dil[:, DIL_WIDTH:2 * DIL_WIDTH].astype(BF16)
    vdT_ref[...] = dil[:, 2 * DIL_WIDTH:].T.astype(BF16)

    gates = _dot_nt(xb, win_ref[C_GATE:IN_AUG, :]) + bg_ref[...]
    g_ref[...] = jax.nn.sigmoid(gates).astype(BF16)


def _proj_call(x2, win, wqT, wk, wvT, gq, gkv, bg, cq, sq, ck, sk):
    T = x2.shape[0]
    tm = ROW_TILE
    tiles_per_seq = SEQ // tm
    const = lambda i: (0, 0)
    row = lambda i: (i, 0)
    col = lambda i: (0, i)
    pos_col = lambda i: (0, i % tiles_per_seq)
    pos_row = lambda i: (i % tiles_per_seq, 0)

    def wspec(a):
        return pl.BlockSpec(a.shape, const, pipeline_mode=pl.Buffered(1))

    out_shapes = (
        jax.ShapeDtypeStruct((MLA_HEADS * HEAD_PAD, T), BF16),
        jax.ShapeDtypeStruct((T, MLA_HEADS * HEAD_PAD), BF16),
        jax.ShapeDtypeStruct((MLA_WIDTH, T), BF16),
        jax.ShapeDtypeStruct((DIL_WIDTH, T), BF16),
        jax.ShapeDtypeStruct((T, DIL_WIDTH), BF16),
        jax.ShapeDtypeStruct((DIL_WIDTH, T), BF16),
        jax.ShapeDtypeStruct((T, N_BRANCH * D_MODEL), BF16),
    )
    out_specs = (
        pl.BlockSpec((MLA_HEADS * HEAD_PAD, tm), col),
        pl.BlockSpec((tm, MLA_HEADS * HEAD_PAD), row),
        pl.BlockSpec((MLA_WIDTH, tm), col),
        pl.BlockSpec((DIL_WIDTH, tm), col),
        pl.BlockSpec((tm, DIL_WIDTH), row),
        pl.BlockSpec((DIL_WIDTH, tm), col),
        pl.BlockSpec((tm, N_BRANCH * D_MODEL), row),
    )
    in_specs = [
        pl.BlockSpec((tm, D_MODEL), row),
        wspec(win), wspec(wqT), wspec(wk), wspec(wvT), wspec(gq), wspec(gkv), wspec(bg),
        pl.BlockSpec((ROPE_HALF, tm), pos_col),
        pl.BlockSpec((ROPE_HALF, tm), pos_col),
        pl.BlockSpec((tm, LANES), pos_row),
        pl.BlockSpec((tm, LANES), pos_row),
    ]
    return pl.pallas_call(
        _proj_kernel,
        grid=(T // tm,),
        in_specs=in_specs,
        out_specs=out_specs,
        out_shape=out_shapes,
        compiler_params=pltpu.CompilerParams(
            dimension_semantics=("arbitrary",), vmem_limit_bytes=VMEM_LIMIT),
        name="proj",
    )(x2, win, wqT, wk, wvT, gq, gkv, bg, cq, sq, ck, sk)


class _Head(NamedTuple):
    qT: object
    keys: object
    valsT: object
    oT: object
    bias_chunk: object


def _causal_attention(heads, s_refs, exp_dtype):
    def qcols(i):
        return slice(i * ATT_Q, (i + 1) * ATT_Q)

    def krows(j):
        return slice(j * ATT_K, (j + 1) * ATT_K)

    def first_col(i, j):
        return max(0, j * ATT_K - i * ATT_Q)

    def widen(x, lo, fill):
        if lo == 0:
            return x
        return jnp.concatenate([jnp.full((x.shape[0], lo), fill, x.dtype), x], axis=1)

    def s_of(item):
        g, i = item
        return s_refs[(g % SCORE_SETS) * N_Q_TILES + i]

    def scores(item, j, q_tile):
        hd, i = heads[item[0]], item[1]
        lo = first_col(i, j)
        sT = _dot(hd.keys[krows(j), :], q_tile[:, lo:])
        b = hd.bias_chunk(i, j)
        if b is not None:
            sT = sT + b[:, lo:]
        s_of(item)[krows(j), lo:] = sT
        return widen(jnp.max(sT, axis=0, keepdims=True), lo, NEG)

    def weights(item, j, m):
        hd, i = heads[item[0]], item[1]
        lo = first_col(i, j)
        p = jnp.exp2((s_of(item)[krows(j), lo:] - m[:, lo:]).astype(exp_dtype))
        return widen(_dot(hd.valsT[:, krows(j)], p.astype(BF16)), lo, 0.0)

    def fold(op, old, new):
        return new if old is None else op(old, new)

    def n_chunks(item):
        return _n_chunks(item[1]) if item is not None else 0

    def q_of(item):
        return heads[item[0]].qT[:, qcols(item[1])] if item is not None else None

    def tiles(group):
        return range(N_Q_TILES) if group % 2 == 0 else reversed(range(N_Q_TILES))

    items = [(g, i) for group, base in enumerate(range(0, len(heads), HEAD_GROUP))
             for i in tiles(group) for g in range(base, base + HEAD_GROUP)]
    for a, (ga, ia) in enumerate(items):
        for gb, ib in items[a + 1:a + SCORE_LEAD]:
            assert (ga % SCORE_SETS, ia) != (gb % SCORE_SETS, ib), "score buffer reused too soon"
    m = []
    for item in items[:SCORE_LEAD]:
        q_tile, mi = q_of(item), None
        for j in range(n_chunks(item)):
            mi = fold(jnp.maximum, mi, scores(item, j, q_tile))
        m.append(mi)
    for pos, item in enumerate(items):
        ahead = items[pos + SCORE_LEAD] if pos + SCORE_LEAD < len(items) else None
        q_ahead, m_ahead, acc = q_of(ahead), None, None
        for j in range(max(n_chunks(item), n_chunks(ahead))):
            if j < n_chunks(item):
                acc = fold(jnp.add, acc, weights(item, j, m[pos]))
            if j < n_chunks(ahead):
                m_ahead = fold(jnp.maximum, m_ahead, scores(ahead, j, q_ahead))
        m.append(m_ahead)
        hd, i = heads[item[0]], item[1]
        d_v = hd.oT.shape[0]
        hd.oT[:, qcols(i)] = (acc[:d_v, :] / acc[d_v:d_v + 1, :]).astype(BF16)


def _n_chunks(i):
    return (i + 1) * (ATT_Q // ATT_K)


def _causal_bias(chunk_offset):
    key = lax.broadcasted_iota(jnp.int32, (ATT_K, ATT_Q), 0) + chunk_offset
    qry = lax.broadcasted_iota(jnp.int32, (ATT_K, ATT_Q), 1)
    return jnp.where(key > qry, NEG, 0.0).astype(F32)


ONES_ROWS = 16


def _values_with_ones(vT, va_ref):
    d_v = vT.shape[0]
    va_ref[0:d_v, :] = vT
    va_ref[d_v:, :] = jnp.ones((ONES_ROWS, va_ref.shape[1]), BF16)


def _att_scratch(d_v):
    return [pltpu.VMEM((HEADS_PER_STEP, d_v + ONES_ROWS, SEQ), BF16)] + [
        pltpu.VMEM((_n_chunks(i) * ATT_K, ATT_Q), F32)
        for _ in range(SCORE_SETS) for i in range(N_Q_TILES)]


def _mla_kernel(qT_ref, k_ref, vT_ref, oT_ref, va_ref, *s_refs):
    per_tile = ATT_Q // ATT_K
    masks = [_causal_bias(c * ATT_K) for c in range(per_tile)]

    def bias_chunk(i, j):
        c = j - i * per_tile
        return masks[c] if c >= 0 else None

    heads = []
    for g in range(HEADS_PER_STEP):
        qk = pl.ds(g * HEAD_PAD, HEAD_PAD)
        vo = pl.ds(g * MLA_V, MLA_V)
        _values_with_ones(vT_ref[vo, :], va_ref.at[g])
        heads.append(_Head(qT_ref.at[qk, :], k_ref.at[:, qk], va_ref.at[g],
                           oT_ref.at[vo, :], bias_chunk))
    _causal_attention(heads, s_refs, F32)


def _mla_call(qT, k, vT, batch):
    T = k.shape[0]
    g = HEADS_PER_STEP
    return pl.pallas_call(
        _mla_kernel,
        grid=(batch, MLA_HEADS // g),
        in_specs=[
            pl.BlockSpec((g * HEAD_PAD, SEQ), lambda b, h: (h, b)),
            pl.BlockSpec((SEQ, g * HEAD_PAD), lambda b, h: (b, h)),
            pl.BlockSpec((g * MLA_V, SEQ), lambda b, h: (h, b)),
        ],
        out_specs=pl.BlockSpec((g * MLA_V, SEQ), lambda b, h: (h, b)),
        out_shape=jax.ShapeDtypeStruct((MLA_WIDTH, T), BF16),
        scratch_shapes=_att_scratch(MLA_V),
        compiler_params=pltpu.CompilerParams(
            dimension_semantics=("arbitrary", "arbitrary"), vmem_limit_bytes=VMEM_LIMIT),
        name="mla_attn",
    )(qT, k, vT)


N_DIL_OFFSETS = SEQ // ATT_K + 1


def _dil_kernel(slopes_ref, qT_ref, k_ref, vT_ref, dist_ref, logc_ref, oT_ref,
                q2_ref, bias_ref, va_ref, *s_refs):
    per_tile = ATT_Q // ATT_K
    d = DIL_HEAD_DIM

    @pl.when(pl.program_id(1) == 0)
    def _():
        for g in range(HEADS_PER_STEP):
            slope = slopes_ref[pl.program_id(0) * HEADS_PER_STEP + g]
            bias_ref[g] = logc_ref[...] - slope * dist_ref[...]

    heads = []
    for g in range(HEADS_PER_STEP):
        rows = pl.ds(g * d, d)
        pair = pl.ds((g * d // LANES) * LANES, LANES)
        q2_ref[g] = jnp.zeros(q2_ref.shape[1:], BF16)
        q2_ref[g, pl.ds(g * d % LANES, d), :] = qT_ref[rows, :]
        _values_with_ones(vT_ref[rows, :], va_ref.at[g])

        def bias_chunk(i, j, g=g):
            return jnp.concatenate(
                [bias_ref[g, i * per_tile + c - j + 1] for c in range(per_tile)], axis=1)

        heads.append(_Head(q2_ref.at[g], k_ref.at[:, pair], va_ref.at[g],
                           oT_ref.at[rows, :], bias_chunk))
    _causal_attention(heads, s_refs, BF16)


def _dil_call(qdT, kd, vdT, batch):
    T = kd.shape[0]
    g = HEADS_PER_STEP
    width = g * DIL_HEAD_DIM
    assert width % LANES == 0
    slopes = jnp.asarray(
        [LOG2E * 2.0 ** (-8.0 * (i + 1) / DIL_HEADS) for i in range(DIL_HEADS)], F32)
    dist, logc = _dilated_tables()
    table = pl.BlockSpec(dist.shape, lambda h, b: (0, 0, 0), pipeline_mode=pl.Buffered(1))
    return pl.pallas_call(
        _dil_kernel,
        grid=(DIL_HEADS // g, batch),
        in_specs=[
            pl.BlockSpec(memory_space=pltpu.SMEM),
            pl.BlockSpec((width, SEQ), lambda h, b: (h, b)),
            pl.BlockSpec((SEQ, width), lambda h, b: (b, h)),
            pl.BlockSpec((width, SEQ), lambda h, b: (h, b)),
            table, table,
        ],
        out_specs=pl.BlockSpec((width, SEQ), lambda h, b: (h, b)),
        out_shape=jax.ShapeDtypeStruct((DIL_WIDTH, T), BF16),
        scratch_shapes=[pltpu.VMEM((g, LANES, SEQ), BF16),
                        pltpu.VMEM((g, N_DIL_OFFSETS, ATT_K, ATT_K), F32)]
        + _att_scratch(DIL_HEAD_DIM),
        compiler_params=pltpu.CompilerParams(
            dimension_semantics=("arbitrary", "arbitrary"), vmem_limit_bytes=VMEM_LIMIT),
        name="dil_attn",
    )(slopes, qdT, kd, vdT, dist, logc)


def _dilated_tables():
    off = np.arange(-1, N_DIL_OFFSETS - 1)[:, None, None]
    key = np.arange(ATT_K)[None, :, None]
    qry = np.arange(ATT_K)[None, None, :]
    dist = ATT_K * off + qry - key
    count = np.zeros(dist.shape, np.int32)
    for window, dilation in DIL_PATTERNS:
        count += ((dist >= 0) & (dist % dilation == 0) & (dist <= window)).astype(np.int32)
    logc = np.where(count > 0, np.log2(np.maximum(count, 1)), NEG)
    return jnp.asarray(dist, F32), jnp.asarray(logc, F32)


def _post_kernel(alpha, x_ref, oaT_ref, obT_ref, g_ref, woa_ref, wob_ref, wout_ref,
                 l1g_ref, l1b_ref, w1_ref, w2_ref, l2g_ref, l2b_ref, out_ref):
    n = x_ref.shape[0] // POST_ROW_GROUPS
    groups = [pl.ds(r * n, n) for r in range(POST_ROW_GROUPS)]
    ya = [_dot(oaT_ref[:, rows].T, woa_ref[...]) for rows in groups]
    yb = [_dot(obT_ref[:, rows].T, wob_ref[...]) for rows in groups]
    mixed = []
    for r, rows in enumerate(groups):
        g = g_ref[rows, :].astype(F32)
        mix = (g[:, :D_MODEL] * ya[r] + g[:, D_MODEL:] * yb[r]).astype(BF16)
        mixed.append(_dot(mix, wout_ref[...]))
    h = [_layer_norm(alpha * x_ref[rows, :] + mixed[r], l1g_ref[...], l1b_ref[...])
         for r, rows in enumerate(groups)]
    hb = [t.astype(BF16) for t in h]
    f = [None] * POST_ROW_GROUPS
    for c in range(D_FF // FF_CHUNK):
        cols = slice(c * FF_CHUNK, (c + 1) * FF_CHUNK)
        u = [jnp.maximum(_dot(t, w1_ref[:, cols]), 0.0) for t in hb]
        for r in range(POST_ROW_GROUPS):
            fc = _dot((u[r] * u[r]).astype(BF16), w2_ref[cols, :])
            f[r] = fc if f[r] is None else f[r] + fc
    for r, rows in enumerate(groups):
        out_ref[rows, :] = _layer_norm(alpha * h[r] + f[r], l2g_ref[...], l2b_ref[...])


def _post_call(alpha, x2, oaT, obT, g, woa, wob, wout, l1g, l1b, w1, w2, l2g, l2b):
    T = x2.shape[0]
    tm = ROW_TILE
    const = lambda i: (0, 0)
    row = lambda i: (i, 0)
    col = lambda i: (0, i)

    def wspec(a):
        return pl.BlockSpec(a.shape, const, pipeline_mode=pl.Buffered(1))

    return pl.pallas_call(
        functools.partial(_post_kernel, alpha),
        grid=(T // tm,),
        in_specs=[
            pl.BlockSpec((tm, D_MODEL), row),
            pl.BlockSpec((MLA_WIDTH, tm), col),
            pl.BlockSpec((DIL_WIDTH, tm), col),
            pl.BlockSpec((tm, N_BRANCH * D_MODEL), row),
            wspec(woa), wspec(wob), wspec(wout), wspec(l1g), wspec(l1b),
            wspec(w1), wspec(w2), wspec(l2g), wspec(l2b),
        ],
        out_specs=pl.BlockSpec((tm, D_MODEL), row),
        out_shape=jax.ShapeDtypeStruct((T, D_MODEL), F32),
        compiler_params=pltpu.CompilerParams(
            dimension_semantics=("arbitrary",), vmem_limit_bytes=VMEM_LIMIT),
        name="post",
    )(x2, oaT, obT, g, woa, wob, wout, l1g, l1b, w1, w2, l2g, l2b)


def _rotate_half_rows(w):
    half = w.shape[0] // 2
    return jnp.concatenate([-w[half:], w[:half]], axis=0)


def _rope_tables():
    inv = jnp.power(ROPE_THETA, -jnp.arange(ROPE_HALF, dtype=F32) / ROPE_HALF)
    ang = jnp.arange(SEQ).astype(F32)[:, None] * inv[None, :]
    cos, sin = jnp.cos(ang), jnp.sin(ang)
    qscale = (MLA_QK ** -0.5) * LOG2E
    cq = (cos * qscale).T
    sq = (sin * qscale).T
    zeros = jnp.zeros((SEQ, MLA_NOPE), F32)
    z32 = jnp.zeros((SEQ, MLA_ROPE), F32)
    ck = jnp.concatenate([zeros, cos, cos, z32], axis=1)
    sk = jnp.concatenate([zeros, z32, sin, sin], axis=1)
    return cq, sq, ck, sk


def _layer(x2, batch, alpha, w_in, b_gate, g_q_a, w_uq, g_kv_a, w_ukv, w_o_mla, w_o_dil,
           w_out, ln1_g, ln1_b, w_ff1, w_ff2, ln2_g, ln2_b):
    s0, s1, s2, s3 = Q_LORA, Q_LORA + KV_LORA, Q_LORA + KV_LORA + MLA_ROPE, \
        Q_LORA + KV_LORA + MLA_ROPE + 3 * DIL_WIDTH
    w_inT = w_in.T.astype(BF16)
    w_kr = w_inT[s1:s2]
    win = jnp.concatenate(
        [w_inT[:s1], jnp.zeros((MLA_NOPE, D_MODEL), BF16), w_kr, _rotate_half_rows(w_kr),
         w_inT[s2:]], axis=0)

    wq = w_uq.reshape(Q_LORA, MLA_HEADS, MLA_QK)
    wq = jnp.pad(wq, ((0, 0), (0, 0), (0, HEAD_PAD - MLA_QK)))
    wqT = wq.reshape(Q_LORA, MLA_HEADS * HEAD_PAD).T.astype(BF16)
    wkv = w_ukv.reshape(KV_LORA, MLA_HEADS, MLA_NOPE + MLA_V)
    wk = jnp.pad(wkv[:, :, :MLA_NOPE], ((0, 0), (0, 0), (0, HEAD_PAD - MLA_NOPE)))
    wk = wk.reshape(KV_LORA, MLA_HEADS * HEAD_PAD).astype(BF16)
    wvT = wkv[:, :, MLA_NOPE:].reshape(KV_LORA, MLA_WIDTH).T.astype(BF16)

    cq, sq, ck, sk = _rope_tables()
    qT, k, vT, qdT, kd, vdT, g = _proj_call(
        x2, win, wqT, wk, wvT, g_q_a[None], g_kv_a[None],
        b_gate.reshape(1, N_BRANCH * D_MODEL), cq, sq, ck, sk)

    oaT = _mla_call(qT, k, vT, batch)
    obT = _dil_call(qdT, kd, vdT, batch)

    return _post_call(alpha, x2, oaT, obT, g, w_o_mla.astype(BF16), w_o_dil.astype(BF16),
                      w_out.astype(BF16), ln1_g[None], ln1_b[None], w_ff1.astype(BF16),
                      w_ff2.astype(BF16), ln2_g[None], ln2_b[None])


def kernel(x, w_in, b_gate, g_q_a, w_uq, g_kv_a, w_ukv, w_o_mla, w_o_dil, w_out,
           ln1_g, ln1_b, w_ff1, w_ff2, ln2_g, ln2_b):
    batch, seq, d = x.shape
    assert (seq, d) == (SEQ, D_MODEL)
    depth = w_in.shape[0]
    alpha = (2 * depth) ** 0.25
    x2 = x.reshape(batch * seq, d)
    for l in range(depth):
        x2 = _layer(x2, batch, alpha, w_in[l], b_gate[l], g_q_a[l], w_uq[l], g_kv_a[l],
                    w_ukv[l], w_o_mla[l], w_o_dil[l], w_out[l], ln1_g[l], ln1_b[l],
                    w_ff1[l], w_ff2[l], ln2_g[l], ln2_b[l])
    return x2.reshape(batch, seq, d)
```

```python
import functools
import math
from typing import NamedTuple

import jax
import jax.numpy as jnp
import numpy as np
from jax import lax
from jax.experimental import pallas as pl
from jax.experimental.pallas import tpu as pltpu

D_MODEL = 1024
SEQ = 2048

MLA_HEADS = 8
MLA_NOPE = 64
MLA_ROPE = 32
MLA_V = 64
Q_LORA = 384
KV_LORA = 256
ROPE_THETA = 10000.0
MLA_QK = MLA_NOPE + MLA_ROPE
MLA_WIDTH = MLA_HEADS * MLA_V
ROPE_HALF = MLA_ROPE // 2

DIL_HEADS = 8
DIL_HEAD_DIM = 64
DIL_PATTERNS = ((128, 1), (512, 4), (2048, 16))
DIL_WIDTH = DIL_HEADS * DIL_HEAD_DIM

N_BRANCH = 2
D_FF = 4 * D_MODEL
LN_EPS = 1e-5
RMS_EPS = 1e-6
NEG = -1e30

LOG2E = math.log2(math.e)

LANES = 128
MXU_DIM = 256
VMEM_LIMIT = 56 * 1024 * 1024

HEAD_PAD = LANES
ROW_TILE = 512
ATT_Q = 2 * MXU_DIM
ATT_K = MXU_DIM
N_Q_TILES = SEQ // ATT_Q
MLA_HEADS_PER_STEP = 8
DIL_HEADS_PER_STEP = 4
HEAD_GROUP = 2
SCORE_SETS = 2
SCORE_LEAD = 2
FF_CHUNK = 1024
POST_ROW_GROUPS = 2

C_QA = 0
C_KVA = C_QA + Q_LORA
C_KR = C_KVA + KV_LORA
C_QD = C_KR + LANES
C_KD = C_QD + DIL_WIDTH
C_VD = C_KD + DIL_WIDTH
C_GATE = C_VD + DIL_WIDTH
IN_AUG = C_GATE + N_BRANCH * D_MODEL

BF16 = jnp.bfloat16
F32 = jnp.float32


def _dot(a, b):
    return jnp.dot(a, b, preferred_element_type=F32)


def _dot_nt(a, b):
    return lax.dot_general(a, b, (((1,), (1,)), ((), ())), preferred_element_type=F32)


def _rms(t, g):
    return t * lax.rsqrt(jnp.mean(t * t, axis=-1, keepdims=True) + RMS_EPS) * g


def _layer_norm(t, g, b):
    mu = jnp.mean(t, axis=-1, keepdims=True)
    d = t - mu
    var = jnp.mean(d * d, axis=-1, keepdims=True)
    return d * lax.rsqrt(var + LN_EPS) * g + b


def _proj_kernel(x_ref, win_ref, wqT_ref, wk_ref, wvT_ref, gq_ref, gkv_ref, bg_ref,
                 cq_ref, sq_ref, ck_ref, sk_ref,
                 qT_ref, k_ref, vT_ref, qdT_ref, kd_ref, vdT_ref, g_ref):
    xb = x_ref[...].astype(BF16)

    lat = _dot_nt(xb, win_ref[C_QA:C_QD, :])
    qn = _rms(lat[:, C_QA:C_KVA], gq_ref[...])
    kvn = _rms(lat[:, C_KVA:C_KR], gkv_ref[...])
    qnT = qn.T.astype(BF16)
    kvnT = kvn.T.astype(BF16)

    qT = _dot(wqT_ref[...], qnT)
    cq = cq_ref[...]
    sq = sq_ref[...]
    qscale = (MLA_QK ** -0.5) * LOG2E
    for h in range(MLA_HEADS):
        b = h * HEAD_PAD
        qT_ref[b:b + MLA_NOPE, :] = (qT[b:b + MLA_NOPE, :] * qscale).astype(BF16)
        t1 = qT[b + MLA_NOPE:b + MLA_NOPE + ROPE_HALF, :]
        t2 = qT[b + MLA_NOPE + ROPE_HALF:b + MLA_QK, :]
        qT_ref[b + MLA_NOPE:b + MLA_NOPE + ROPE_HALF, :] = (t1 * cq - t2 * sq).astype(BF16)
        qT_ref[b + MLA_NOPE + ROPE_HALF:b + MLA_QK, :] = (t1 * sq + t2 * cq).astype(BF16)
        qT_ref[b + MLA_QK:b + HEAD_PAD, :] = jnp.zeros((HEAD_PAD - MLA_QK, qT.shape[1]), BF16)

    kr = lat[:, C_KR:C_QD]
    kr = kr * ck_ref[...] + pltpu.roll(kr * sk_ref[...], LANES - MLA_ROPE, axis=1)
    k = _dot(kvn.astype(BF16), wk_ref[...]) + jnp.concatenate([kr] * MLA_HEADS, axis=1)
    k_ref[...] = k.astype(BF16)
    vT_ref[...] = _dot(wvT_ref[...], kvnT).astype(BF16)

    dil = _dot_nt(xb, win_ref[C_QD:C_GATE, :])
    dscale = (DIL_HEAD_DIM ** -0.5) * LOG2E
    qdT_ref[...] = (dil[:, :DIL_WIDTH] * dscale).T.astype(BF16)
    kd_ref[...] = dil[:, DIL_WIDTH:2 * DIL_WIDTH].astype(BF16)
    vdT_ref[...] = dil[:, 2 * DIL_WIDTH:].T.astype(BF16)

    gates = _dot_nt(xb, win_ref[C_GATE:IN_AUG, :]) + bg_ref[...]
    g_ref[...] = jax.nn.sigmoid(gates).astype(BF16)


def _proj_call(x2, win, wqT, wk, wvT, gq, gkv, bg, cq, sq, ck, sk):
    T = x2.shape[0]
    tm = ROW_TILE
    tiles_per_seq = SEQ // tm
    const = lambda i: (0, 0)
    row = lambda i: (i, 0)
    col = lambda i: (0, i)
    pos_col = lambda i: (0, i % tiles_per_seq)
    pos_row = lambda i: (i % tiles_per_seq, 0)

    def wspec(a):
        return pl.BlockSpec(a.shape, const, pipeline_mode=pl.Buffered(1))

    out_shapes = (
        jax.ShapeDtypeStruct((MLA_HEADS * HEAD_PAD, T), BF16),
        jax.ShapeDtypeStruct((T, MLA_HEADS * HEAD_PAD), BF16),
        jax.ShapeDtypeStruct((MLA_WIDTH, T), BF16),
        jax.ShapeDtypeStruct((DIL_WIDTH, T), BF16),
        jax.ShapeDtypeStruct((T, DIL_WIDTH), BF16),
        jax.ShapeDtypeStruct((DIL_WIDTH, T), BF16),
        jax.ShapeDtypeStruct((T, N_BRANCH * D_MODEL), BF16),
    )
    out_specs = (
        pl.BlockSpec((MLA_HEADS * HEAD_PAD, tm), col),
        pl.BlockSpec((tm, MLA_HEADS * HEAD_PAD), row),
        pl.BlockSpec((MLA_WIDTH, tm), col),
        pl.BlockSpec((DIL_WIDTH, tm), col),
        pl.BlockSpec((tm, DIL_WIDTH), row),
        pl.BlockSpec((DIL_WIDTH, tm), col),
        pl.BlockSpec((tm, N_BRANCH * D_MODEL), row),
    )
    in_specs = [
        pl.BlockSpec((tm, D_MODEL), row),
        wspec(win), wspec(wqT), wspec(wk), wspec(wvT), wspec(gq), wspec(gkv), wspec(bg),
        pl.BlockSpec((ROPE_HALF, tm), pos_col),
        pl.BlockSpec((ROPE_HALF, tm), pos_col),
        pl.BlockSpec((tm, LANES), pos_row),
        pl.BlockSpec((tm, LANES), pos_row),
    ]
    return pl.pallas_call(
        _proj_kernel,
        grid=(T // tm,),
        in_specs=in_specs,
        out_specs=out_specs,
        out_shape=out_shapes,
        compiler_params=pltpu.CompilerParams(
            dimension_semantics=("arbitrary",), vmem_limit_bytes=VMEM_LIMIT),
        name="proj",
    )(x2, win, wqT, wk, wvT, gq, gkv, bg, cq, sq, ck, sk)


class _Head(NamedTuple):
    qT: object
    keys: object
    valsT: object
    oT: object
    bias_chunk: object


def _causal_attention(heads, s_refs, exp_dtype):
    def qcols(i):
        return slice(i * ATT_Q, (i + 1) * ATT_Q)

    def krows(j):
        return slice(j * ATT_K, (j + 1) * ATT_K)

    def first_col(i, j):
        return max(0, j * ATT_K - i * ATT_Q)

    def widen(x, lo, fill):
        if lo == 0:
            return x
        return jnp.concatenate([jnp.full((x.shape[0], lo), fill, x.dtype), x], axis=1)

    def s_of(item):
        g, i = item
        return s_refs[(g % SCORE_SETS) * N_Q_TILES + i]

    def scores(item, j, q_tile):
        hd, i = heads[item[0]], item[1]
        lo = first_col(i, j)
        sT = _dot(hd.keys[krows(j), :], q_tile[:, lo:])
        b = hd.bias_chunk(i, j)
        if b is not None:
            sT = sT + b[:, lo:]
        s_of(item)[krows(j), lo:] = sT
        return widen(jnp.max(sT, axis=0, keepdims=True), lo, NEG)

    def weights(item, j, m):
        hd, i = heads[item[0]], item[1]
        lo = first_col(i, j)
        p = jnp.exp2((s_of(item)[krows(j), lo:] - m[:, lo:]).astype(exp_dtype))
        return widen(_dot(hd.valsT[:, krows(j)], p.astype(BF16)), lo, 0.0)

    def fold(op, old, new):
        return new if old is None else op(old, new)

    def n_chunks(item):
        return _n_chunks(item[1]) if item is not None else 0

    def q_of(item):
        return heads[item[0]].qT[:, qcols(item[1])] if item is not None else None

    def tiles(group):
        return range(N_Q_TILES) if group % 2 == 0 else reversed(range(N_Q_TILES))

    items = [(g, i) for group, base in enumerate(range(0, len(heads), HEAD_GROUP))
             for i in tiles(group) for g in range(base, base + HEAD_GROUP)]
    for a, (ga, ia) in enumerate(items):
        for gb, ib in items[a + 1:a + SCORE_LEAD]:
            assert (ga % SCORE_SETS, ia) != (gb % SCORE_SETS, ib), "score buffer reused too soon"
    m = []
    for item in items[:SCORE_LEAD]:
        q_tile, mi = q_of(item), None
        for j in range(n_chunks(item)):
            mi = fold(jnp.maximum, mi, scores(item, j, q_tile))
        m.append(mi)
    for pos, item in enumerate(items):
        ahead = items[pos + SCORE_LEAD] if pos + SCORE_LEAD < len(items) else None
        q_ahead, m_ahead, acc = q_of(ahead), None, None
        for j in range(max(n_chunks(item), n_chunks(ahead))):
            if j < n_chunks(item):
                acc = fold(jnp.add, acc, weights(item, j, m[pos]))
            if j < n_chunks(ahead):
                m_ahead = fold(jnp.maximum, m_ahead, scores(ahead, j, q_ahead))
        m.append(m_ahead)
        hd, i = heads[item[0]], item[1]
        d_v = hd.oT.shape[0]
        hd.oT[:, qcols(i)] = (acc[:d_v, :] / acc[d_v:d_v + 1, :]).astype(BF16)


def _n_chunks(i):
    return (i + 1) * (ATT_Q // ATT_K)


def _causal_bias(chunk_offset):
    key = lax.broadcasted_iota(jnp.int32, (ATT_K, ATT_Q), 0) + chunk_offset
    qry = lax.broadcasted_iota(jnp.int32, (ATT_K, ATT_Q), 1)
    return jnp.where(key > qry, NEG, 0.0).astype(F32)


ONES_ROWS = 16


def _values_with_ones(vT, va_ref):
    d_v = vT.shape[0]
    va_ref[0:d_v, :] = vT
    va_ref[d_v:, :] = jnp.ones((ONES_ROWS, va_ref.shape[1]), BF16)


def _att_scratch(d_v, heads_per_step):
    return [pltpu.VMEM((heads_per_step, d_v + ONES_ROWS, SEQ), BF16)] + [
        pltpu.VMEM((_n_chunks(i) * ATT_K, ATT_Q), F32)
        for _ in range(SCORE_SETS) for i in range(N_Q_TILES)]


def _mla_kernel(qT_ref, k_ref, vT_ref, oT_ref, va_ref, *s_refs):
    per_tile = ATT_Q // ATT_K
    masks = [_causal_bias(c * ATT_K) for c in range(per_tile)]

    def bias_chunk(i, j):
        c = j - i * per_tile
        return masks[c] if c >= 0 else None

    heads = []
    for g in range(MLA_HEADS_PER_STEP):
        qk = pl.ds(g * HEAD_PAD, HEAD_PAD)
        vo = pl.ds(g * MLA_V, MLA_V)
        _values_with_ones(vT_ref[vo, :], va_ref.at[g])
        heads.append(_Head(qT_ref.at[qk, :], k_ref.at[:, qk], va_ref.at[g],
                           oT_ref.at[vo, :], bias_chunk))
    _causal_attention(heads, s_refs, F32)


def _mla_call(qT, k, vT, batch):
    T = k.shape[0]
    g = MLA_HEADS_PER_STEP
    return pl.pallas_call(
        _mla_kernel,
        grid=(batch, MLA_HEADS // g),
        in_specs=[
            pl.BlockSpec((g * HEAD_PAD, SEQ), lambda b, h: (h, b)),
            pl.BlockSpec((SEQ, g * HEAD_PAD), lambda b, h: (b, h)),
            pl.BlockSpec((g * MLA_V, SEQ), lambda b, h: (h, b)),
        ],
        out_specs=pl.BlockSpec((g * MLA_V, SEQ), lambda b, h: (h, b)),
        out_shape=jax.ShapeDtypeStruct((MLA_WIDTH, T), BF16),
        scratch_shapes=_att_scratch(MLA_V, MLA_HEADS_PER_STEP),
        compiler_params=pltpu.CompilerParams(
            dimension_semantics=("arbitrary", "arbitrary"), vmem_limit_bytes=VMEM_LIMIT),
        name="mla_attn",
    )(qT, k, vT)


N_DIL_OFFSETS = SEQ // ATT_K + 1


def _dil_kernel(slopes_ref, qT_ref, k_ref, vT_ref, dist_ref, logc_ref, oT_ref,
                q2_ref, bias_ref, va_ref, *s_refs):
    per_tile = ATT_Q // ATT_K
    d = DIL_HEAD_DIM

    @pl.when(pl.program_id(1) == 0)
    def _():
        for g in range(DIL_HEADS_PER_STEP):
            slope = slopes_ref[pl.program_id(0) * DIL_HEADS_PER_STEP + g]
            bias_ref[g] = logc_ref[...] - slope * dist_ref[...]

    heads = []
    for g in range(DIL_HEADS_PER_STEP):
        rows = pl.ds(g * d, d)
        pair = pl.ds((g * d // LANES) * LANES, LANES)
        q2_ref[g] = jnp.zeros(q2_ref.shape[1:], BF16)
        q2_ref[g, pl.ds(g * d % LANES, d), :] = qT_ref[rows, :]
        _values_with_ones(vT_ref[rows, :], va_ref.at[g])

        def bias_chunk(i, j, g=g):
            return jnp.concatenate(
                [bias_ref[g, i * per_tile + c - j + 1] for c in range(per_tile)], axis=1)

        heads.append(_Head(q2_ref.at[g], k_ref.at[:, pair], va_ref.at[g],
                           oT_ref.at[rows, :], bias_chunk))
    _causal_attention(heads, s_refs, BF16)


def _dil_call(qdT, kd, vdT, batch):
    T = kd.shape[0]
    g = DIL_HEADS_PER_STEP
    width = g * DIL_HEAD_DIM
    assert width % LANES == 0
    slopes = jnp.asarray(
        [LOG2E * 2.0 ** (-8.0 * (i + 1) / DIL_HEADS) for i in range(DIL_HEADS)], F32)
    dist, logc = _dilated_tables()
    table = pl.BlockSpec(dist.shape, lambda h, b: (0, 0, 0), pipeline_mode=pl.Buffered(1))
    return pl.pallas_call(
        _dil_kernel,
        grid=(DIL_HEADS // g, batch),
        in_specs=[
            pl.BlockSpec(memory_space=pltpu.SMEM),
            pl.BlockSpec((width, SEQ), lambda h, b: (h, b)),
            pl.BlockSpec((SEQ, width), lambda h, b: (b, h)),
            pl.BlockSpec((width, SEQ), lambda h, b: (h, b)),
            table, table,
        ],
        out_specs=pl.BlockSpec((width, SEQ), lambda h, b: (h, b)),
        out_shape=jax.ShapeDtypeStruct((DIL_WIDTH, T), BF16),
        scratch_shapes=[pltpu.VMEM((g, LANES, SEQ), BF16),
                        pltpu.VMEM((g, N_DIL_OFFSETS, ATT_K, ATT_K), F32)]
        + _att_scratch(DIL_HEAD_DIM, DIL_HEADS_PER_STEP),
        compiler_params=pltpu.CompilerParams(
            dimension_semantics=("arbitrary", "arbitrary"), vmem_limit_bytes=VMEM_LIMIT),
        name="dil_attn",
    )(slopes, qdT, kd, vdT, dist, logc)


def _dilated_tables():
    off = np.arange(-1, N_DIL_OFFSETS - 1)[:, None, None]
    key = np.arange(ATT_K)[None, :, None]
    qry = np.arange(ATT_K)[None, None, :]
    dist = ATT_K * off + qry - key
    count = np.zeros(dist.shape, np.int32)
    for window, dilation in DIL_PATTERNS:
        count += ((dist >= 0) & (dist % dilation == 0) & (dist <= window)).astype(np.int32)
    logc = np.where(count > 0, np.log2(np.maximum(count, 1)), NEG)
    return jnp.asarray(dist, F32), jnp.asarray(logc, F32)


def _post_kernel(alpha, x_ref, oaT_ref, obT_ref, g_ref, woa_ref, wob_ref, wout_ref,
                 l1g_ref, l1b_ref, w1_ref, w2_ref, l2g_ref, l2b_ref, out_ref):
    n = x_ref.shape[0] // POST_ROW_GROUPS
    groups = [pl.ds(r * n, n) for r in range(POST_ROW_GROUPS)]
    ya = [_dot(oaT_ref[:, rows].T, woa_ref[...]) for rows in groups]
    yb = [_dot(obT_ref[:, rows].T, wob_ref[...]) for rows in groups]
    mixed = []
    for r, rows in enumerate(groups):
        g = g_ref[rows, :].astype(F32)
        mix = (g[:, :D_MODEL] * ya[r] + g[:, D_MODEL:] * yb[r]).astype(BF16)
        mixed.append(_dot(mix, wout_ref[...]))
    h = [_layer_norm(alpha * x_ref[rows, :] + mixed[r], l1g_ref[...], l1b_ref[...])
         for r, rows in enumerate(groups)]
    hb = [t.astype(BF16) for t in h]
    f = [None] * POST_ROW_GROUPS
    for c in range(D_FF // FF_CHUNK):
        cols = slice(c * FF_CHUNK, (c + 1) * FF_CHUNK)
        u = [jnp.maximum(_dot(t, w1_ref[:, cols]), 0.0) for t in hb]
        for r in range(POST_ROW_GROUPS):
            fc = _dot((u[r] * u[r]).astype(BF16), w2_ref[cols, :])
            f[r] = fc if f[r] is None else f[r] + fc
    for r, rows in enumerate(groups):
        out_ref[rows, :] = _layer_norm(alpha * h[r] + f[r], l2g_ref[...], l2b_ref[...])


def _post_call(alpha, x2, oaT, obT, g, woa, wob, wout, l1g, l1b, w1, w2, l2g, l2b):
    T = x2.shape[0]
    tm = ROW_TILE
    const = lambda i: (0, 0)
    row = lambda i: (i, 0)
    col = lambda i: (0, i)

    def wspec(a):
        return pl.BlockSpec(a.shape, const, pipeline_mode=pl.Buffered(1))

    return pl.pallas_call(
        functools.partial(_post_kernel, alpha),
        grid=(T // tm,),
        in_specs=[
            pl.BlockSpec((tm, D_MODEL), row),
            pl.BlockSpec((MLA_WIDTH, tm), col),
            pl.BlockSpec((DIL_WIDTH, tm), col),
            pl.BlockSpec((tm, N_BRANCH * D_MODEL), row),
            wspec(woa), wspec(wob), wspec(wout), wspec(l1g), wspec(l1b),
            wspec(w1), wspec(w2), wspec(l2g), wspec(l2b),
        ],
        out_specs=pl.BlockSpec((tm, D_MODEL), row),
        out_shape=jax.ShapeDtypeStruct((T, D_MODEL), F32),
        compiler_params=pltpu.CompilerParams(
            dimension_semantics=("arbitrary",), vmem_limit_bytes=VMEM_LIMIT),
        name="post",
    )(x2, oaT, obT, g, woa, wob, wout, l1g, l1b, w1, w2, l2g, l2b)


def _rotate_half_rows(w):
    half = w.shape[0] // 2
    return jnp.concatenate([-w[half:], w[:half]], axis=0)


def _rope_tables():
    inv = jnp.power(ROPE_THETA, -jnp.arange(ROPE_HALF, dtype=F32) / ROPE_HALF)
    ang = jnp.arange(SEQ).astype(F32)[:, None] * inv[None, :]
    cos, sin = jnp.cos(ang), jnp.sin(ang)
    qscale = (MLA_QK ** -0.5) * LOG2E
    cq = (cos * qscale).T
    sq = (sin * qscale).T
    zeros = jnp.zeros((SEQ, MLA_NOPE), F32)
    z32 = jnp.zeros((SEQ, MLA_ROPE), F32)
    ck = jnp.concatenate([zeros, cos, cos, z32], axis=1)
    sk = jnp.concatenate([zeros, z32, sin, sin], axis=1)
    return cq, sq, ck, sk


def _layer(x2, batch, alpha, w_in, b_gate, g_q_a, w_uq, g_kv_a, w_ukv, w_o_mla, w_o_dil,
           w_out, ln1_g, ln1_b, w_ff1, w_ff2, ln2_g, ln2_b):
    s0, s1, s2, s3 = Q_LORA, Q_LORA + KV_LORA, Q_LORA + KV_LORA + MLA_ROPE, \
        Q_LORA + KV_LORA + MLA_ROPE + 3 * DIL_WIDTH
    w_inT = w_in.T.astype(BF16)
    w_kr = w_inT[s1:s2]
    win = jnp.concatenate(
        [w_inT[:s1], jnp.zeros((MLA_NOPE, D_MODEL), BF16), w_kr, _rotate_half_rows(w_kr),
         w_inT[s2:]], axis=0)

    wq = w_uq.reshape(Q_LORA, MLA_HEADS, MLA_QK)
    wq = jnp.pad(wq, ((0, 0), (0, 0), (0, HEAD_PAD - MLA_QK)))
    wqT = wq.reshape(Q_LORA, MLA_HEADS * HEAD_PAD).T.astype(BF16)
    wkv = w_ukv.reshape(KV_LORA, MLA_HEADS, MLA_NOPE + MLA_V)
    wk = jnp.pad(wkv[:, :, :MLA_NOPE], ((0, 0), (0, 0), (0, HEAD_PAD - MLA_NOPE)))
    wk = wk.reshape(KV_LORA, MLA_HEADS * HEAD_PAD).astype(BF16)
    wvT = wkv[:, :, MLA_NOPE:].reshape(KV_LORA, MLA_WIDTH).T.astype(BF16)

    cq, sq, ck, sk = _rope_tables()
    qT, k, vT, qdT, kd, vdT, g = _proj_call(
        x2, win, wqT, wk, wvT, g_q_a[None], g_kv_a[None],
        b_gate.reshape(1, N_BRANCH * D_MODEL), cq, sq, ck, sk)

    oaT = _mla_call(qT, k, vT, batch)
    obT = _dil_call(qdT, kd, vdT, batch)

    return _post_call(alpha, x2, oaT, obT, g, w_o_mla.astype(BF16), w_o_dil.astype(BF16),
                      w_out.astype(BF16), ln1_g[None], ln1_b[None], w_ff1.astype(BF16),
                      w_ff2.astype(BF16), ln2_g[None], ln2_b[None])


def kernel(x, w_in, b_gate, g_q_a, w_uq, g_kv_a, w_ukv, w_o_mla, w_o_dil, w_out,
           ln1_g, ln1_b, w_ff1, w_ff2, ln2_g, ln2_b):
    batch, seq, d = x.shape
    assert (seq, d) == (SEQ, D_MODEL)
    depth = w_in.shape[0]
    alpha = (2 * depth) ** 0.25
    x2 = x.reshape(batch * seq, d)
    for l in range(depth):
        x2 = _layer(x2, batch, alpha, w_in[l], b_gate[l], g_q_a[l], w_uq[l], g_kv_a[l],
                    w_ukv[l], w_o_mla[l], w_o_dil[l], w_out[l], ln1_g[l], ln1_b[l],
                    w_ff1[l], w_ff2[l], ln2_g[l], ln2_b[l])
    return x2.reshape(batch, seq, d)
```

```python
import functools
import math
from typing import NamedTuple

import jax
import jax.numpy as jnp
import numpy as np
from jax import lax
from jax.experimental import pallas as pl
from jax.experimental.pallas import tpu as pltpu

D_MODEL = 1024
SEQ = 2048

MLA_HEADS = 8
MLA_NOPE = 64
MLA_ROPE = 32
MLA_V = 64
Q_LORA = 384
KV_LORA = 256
ROPE_THETA = 10000.0
MLA_QK = MLA_NOPE + MLA_ROPE
MLA_WIDTH = MLA_HEADS * MLA_V
ROPE_HALF = MLA_ROPE // 2

DIL_HEADS = 8
DIL_HEAD_DIM = 64
DIL_PATTERNS = ((128, 1), (512, 4), (2048, 16))
DIL_WIDTH = DIL_HEADS * DIL_HEAD_DIM

N_BRANCH = 2
D_FF = 4 * D_MODEL
LN_EPS = 1e-5
RMS_EPS = 1e-6
NEG = -1e30

LOG2E = math.log2(math.e)

LANES = 128
MXU_DIM = 256
VMEM_LIMIT = 56 * 1024 * 1024

HEAD_PAD = LANES
ROW_TILE = 512
ATT_Q = 2 * MXU_DIM
ATT_K = MXU_DIM
N_Q_TILES = SEQ // ATT_Q
HEADS_PER_STEP = 4
HEAD_GROUP = 2
SCORE_SETS = 2
SCORE_LEAD = 2
FF_CHUNK = 1024
POST_ROW_GROUPS = 2

C_QA = 0
C_KVA = C_QA + Q_LORA
C_KR = C_KVA + KV_LORA
C_QD = C_KR + LANES
C_KD = C_QD + DIL_WIDTH
C_VD = C_KD + DIL_WIDTH
C_GATE = C_VD + DIL_WIDTH
IN_AUG = C_GATE + N_BRANCH * D_MODEL

BF16 = jnp.bfloat16
F32 = jnp.float32


def _dot(a, b):
    return jnp.dot(a, b, preferred_element_type=F32)


def _dot_nt(a, b):
    return lax.dot_general(a, b, (((1,), (1,)), ((), ())), preferred_element_type=F32)


def _rms(t, g):
    return t * lax.rsqrt(jnp.mean(t * t, axis=-1, keepdims=True) + RMS_EPS) * g


def _layer_norm(t, g, b):
    mu = jnp.mean(t, axis=-1, keepdims=True)
    d = t - mu
    var = jnp.mean(d * d, axis=-1, keepdims=True)
    return d * lax.rsqrt(var + LN_EPS) * g + b


def _proj_kernel(x_ref, win_ref, wqT_ref, wk_ref, wvT_ref, gq_ref, gkv_ref, bg_ref,
                 cq_ref, sq_ref, ck_ref, sk_ref,
                 qT_ref, k_ref, vT_ref, qdT_ref, kd_ref, vdT_ref, g_ref):
    xb = x_ref[...].astype(BF16)

    lat = _dot_nt(xb, win_ref[C_QA:C_QD, :])
    qn = _rms(lat[:, C_QA:C_KVA], gq_ref[...])
    kvn = _rms(lat[:, C_KVA:C_KR], gkv_ref[...])
    qnT = qn.T.astype(BF16)
    kvnT = kvn.T.astype(BF16)

    qT = _dot(wqT_ref[...], qnT)
    cq = cq_ref[...]
    sq = sq_ref[...]
    qscale = (MLA_QK ** -0.5) * LOG2E
    for h in range(MLA_HEADS):
        b = h * HEAD_PAD
        qT_ref[b:b + MLA_NOPE, :] = (qT[b:b + MLA_NOPE, :] * qscale).astype(BF16)
        t1 = qT[b + MLA_NOPE:b + MLA_NOPE + ROPE_HALF, :]
        t2 = qT[b + MLA_NOPE + ROPE_HALF:b + MLA_QK, :]
        qT_ref[b + MLA_NOPE:b + MLA_NOPE + ROPE_HALF, :] = (t1 * cq - t2 * sq).astype(BF16)
        qT_ref[b + MLA_NOPE + ROPE_HALF:b + MLA_QK, :] = (t1 * sq + t2 * cq).astype(BF16)
        qT_ref[b + MLA_QK:b + HEAD_PAD, :] = jnp.zeros((HEAD_PAD - MLA_QK, qT.shape[1]), BF16)

    kr = lat[:, C_KR:C_QD]
    kr = kr * ck_ref[...] + pltpu.roll(kr * sk_ref[...], LANES - MLA_ROPE, axis=1)
    k = _dot(kvn.astype(BF16), wk_ref[...]) + jnp.concatenate([kr] * MLA_HEADS, axis=1)
    k_ref[...] = k.astype(BF16)
    vT_ref[...] = _dot(wvT_ref[...], kvnT).astype(BF16)

    dil = _dot_nt(xb, win_ref[C_QD:C_GATE, :])
    dscale = (DIL_HEAD_DIM ** -0.5) * LOG2E
    qdT_ref[...] = (dil[:, :DIL_WIDTH] * dscale).T.astype(BF16)
    kd_ref[...] = dil[:, DIL_WIDTH:2 * DIL_WIDTH].astype(BF16)
    vdT_ref[...] = dil[:, 2 * DIL_WIDTH:].T.astype(BF16)

    gates = _dot_nt(xb, win_ref[C_GATE:IN_AUG, :]) + bg_ref[...]
    g_ref[...] = jax.nn.sigmoid(gates).astype(BF16)


def _proj_call(x2, win, wqT, wk, wvT, gq, gkv, bg, cq, sq, ck, sk):
    T = x2.shape[0]
    tm = ROW_TILE
    tiles_per_seq = SEQ // tm
    const = lambda i: (0, 0)
    row = lambda i: (i, 0)
    col = lambda i: (0, i)
    pos_col = lambda i: (0, i % tiles_per_seq)
    pos_row = lambda i: (i % tiles_per_seq, 0)

    def wspec(a):
        return pl.BlockSpec(a.shape, const, pipeline_mode=pl.Buffered(1))

    out_shapes = (
        jax.ShapeDtypeStruct((MLA_HEADS * HEAD_PAD, T), BF16),
        jax.ShapeDtypeStruct((T, MLA_HEADS * HEAD_PAD), BF16),
        jax.ShapeDtypeStruct((MLA_WIDTH, T), BF16),
        jax.ShapeDtypeStruct((DIL_WIDTH, T), BF16),
        jax.ShapeDtypeStruct((T, DIL_WIDTH), BF16),
        jax.ShapeDtypeStruct((DIL_WIDTH, T), BF16),
        jax.ShapeDtypeStruct((T, N_BRANCH * D_MODEL), BF16),
    )
    out_specs = (
        pl.BlockSpec((MLA_HEADS * HEAD_PAD, tm), col),
        pl.BlockSpec((tm, MLA_HEADS * HEAD_PAD), row),
        pl.BlockSpec((MLA_WIDTH, tm), col),
        pl.BlockSpec((DIL_WIDTH, tm), col),
        pl.BlockSpec((tm, DIL_WIDTH), row),
        pl.BlockSpec((DIL_WIDTH, tm), col),
        pl.BlockSpec((tm, N_BRANCH * D_MODEL), row),
    )
    in_specs = [
        pl.BlockSpec((tm, D_MODEL), row),
        wspec(win), wspec(wqT), wspec(wk), wspec(wvT), wspec(gq), wspec(gkv), wspec(bg),
        pl.BlockSpec((ROPE_HALF, tm), pos_col),
        pl.BlockSpec((ROPE_HALF, tm), pos_col),
        pl.BlockSpec((tm, LANES), pos_row),
        pl.BlockSpec((tm, LANES), pos_row),
    ]
    return pl.pallas_call(
        _proj_kernel,
        grid=(T // tm,),
        in_specs=in_specs,
        out_specs=out_specs,
        out_shape=out_shapes,
        compiler_params=pltpu.CompilerParams(
            dimension_semantics=("arbitrary",), vmem_limit_bytes=VMEM_LIMIT),
        name="proj",
    )(x2, win, wqT, wk, wvT, gq, gkv, bg, cq, sq, ck, sk)


class _Head(NamedTuple):
    qT: object
    keys: object
    valsT: object
    oT: object
    bias_chunk: object


def _causal_attention(heads, s_refs, exp_dtype):
    def qcols(i):
        return slice(i * ATT_Q, (i + 1) * ATT_Q)

    def krows(j):
        return slice(j * ATT_K, (j + 1) * ATT_K)

    def first_col(i, j):
        return max(0, j * ATT_K - i * ATT_Q)

    def widen(x, lo, fill):
        if lo == 0:
            return x
        return jnp.concatenate([jnp.full((x.shape[0], lo), fill, x.dtype), x], axis=1)

    def s_of(item):
        g, i = item
        return s_refs[(g % SCORE_SETS) * N_Q_TILES + i]

    def scores(item, j, q_tile):
        hd, i = heads[item[0]], item[1]
        lo = first_col(i, j)
        sT = _dot(hd.keys[krows(j), :], q_tile[:, lo:])
        b = hd.bias_chunk(i, j)
        if b is not None:
            sT = sT + b[:, lo:]
        s_of(item)[krows(j), lo:] = sT
        return widen(jnp.max(sT, axis=0, keepdims=True), lo, NEG)

    def weights(item, j, m, acc):
        hd, i = heads[item[0]], item[1]
        for c0 in range(first_col(i, j), ATT_Q, MXU_DIM):
            cols = slice(c0, c0 + MXU_DIM)
            p = jnp.exp2((s_of(item)[krows(j), cols] - m[:, cols]).astype(exp_dtype))
            acc[c0] = fold(jnp.add, acc.get(c0), _dot(hd.valsT[:, krows(j)], p.astype(BF16)))

    def fold(op, old, new):
        return new if old is None else op(old, new)

    def n_chunks(item):
        return _n_chunks(item[1]) if item is not None else 0

    def q_of(item):
        return heads[item[0]].qT[:, qcols(item[1])] if item is not None else None

    def tiles(group):
        return range(N_Q_TILES) if group % 2 == 0 else reversed(range(N_Q_TILES))

    items = [(g, i) for group, base in enumerate(range(0, len(heads), HEAD_GROUP))
             for i in tiles(group) for g in range(base, base + HEAD_GROUP)]
    for a, (ga, ia) in enumerate(items):
        for gb, ib in items[a + 1:a + SCORE_LEAD]:
            assert (ga % SCORE_SETS, ia) != (gb % SCORE_SETS, ib), "score buffer reused too soon"
    m = []
    for item in items[:SCORE_LEAD]:
        q_tile, mi = q_of(item), None
        for j in range(n_chunks(item)):
            mi = fold(jnp.maximum, mi, scores(item, j, q_tile))
        m.append(mi)
    for pos, item in enumerate(items):
        ahead = items[pos + SCORE_LEAD] if pos + SCORE_LEAD < len(items) else None
        q_ahead, m_ahead, acc = q_of(ahead), None, {}
        for j in range(max(n_chunks(item), n_chunks(ahead))):
            if j < n_chunks(item):
                weights(item, j, m[pos], acc)
            if j < n_chunks(ahead):
                m_ahead = fold(jnp.maximum, m_ahead, scores(ahead, j, q_ahead))
        m.append(m_ahead)
        hd, i = heads[item[0]], item[1]
        d_v = hd.oT.shape[0]
        for c0, a in acc.items():
            cols = slice(i * ATT_Q + c0, i * ATT_Q + c0 + MXU_DIM)
            hd.oT[:, cols] = (a[:d_v, :] / a[d_v:d_v + 1, :]).astype(BF16)


def _n_chunks(i):
    return (i + 1) * (ATT_Q // ATT_K)


def _causal_bias(chunk_offset):
    key = lax.broadcasted_iota(jnp.int32, (ATT_K, ATT_Q), 0) + chunk_offset
    qry = lax.broadcasted_iota(jnp.int32, (ATT_K, ATT_Q), 1)
    return jnp.where(key > qry, NEG, 0.0).astype(F32)


ONES_ROWS = 16


def _values_with_ones(vT, va_ref):
    d_v = vT.shape[0]
    va_ref[0:d_v, :] = vT
    va_ref[d_v:, :] = jnp.ones((ONES_ROWS, va_ref.shape[1]), BF16)


def _att_scratch(d_v):
    return [pltpu.VMEM((HEADS_PER_STEP, d_v + ONES_ROWS, SEQ), BF16)] + [
        pltpu.VMEM((_n_chunks(i) * ATT_K, ATT_Q), F32)
        for _ in range(SCORE_SETS) for i in range(N_Q_TILES)]


def _mla_kernel(qT_ref, k_ref, vT_ref, oT_ref, va_ref, *s_refs):
    per_tile = ATT_Q // ATT_K
    masks = [_causal_bias(c * ATT_K) for c in range(per_tile)]

    def bias_chunk(i, j):
        c = j - i * per_tile
        return masks[c] if c >= 0 else None

    heads = []
    for g in range(HEADS_PER_STEP):
        qk = pl.ds(g * HEAD_PAD, HEAD_PAD)
        vo = pl.ds(g * MLA_V, MLA_V)
        _values_with_ones(vT_ref[vo, :], va_ref.at[g])
        heads.append(_Head(qT_ref.at[qk, :], k_ref.at[:, qk], va_ref.at[g],
                           oT_ref.at[vo, :], bias_chunk))
    _causal_attention(heads, s_refs, F32)


def _mla_call(qT, k, vT, batch):
    T = k.shape[0]
    g = HEADS_PER_STEP
    return pl.pallas_call(
        _mla_kernel,
        grid=(batch, MLA_HEADS // g),
        in_specs=[
            pl.BlockSpec((g * HEAD_PAD, SEQ), lambda b, h: (h, b)),
            pl.BlockSpec((SEQ, g * HEAD_PAD), lambda b, h: (b, h)),
            pl.BlockSpec((g * MLA_V, SEQ), lambda b, h: (h, b)),
        ],
        out_specs=pl.BlockSpec((g * MLA_V, SEQ), lambda b, h: (h, b)),
        out_shape=jax.ShapeDtypeStruct((MLA_WIDTH, T), BF16),
        scratch_shapes=_att_scratch(MLA_V),
        compiler_params=pltpu.CompilerParams(
            dimension_semantics=("arbitrary", "arbitrary"), vmem_limit_bytes=VMEM_LIMIT),
        name="mla_attn",
    )(qT, k, vT)


N_DIL_OFFSETS = SEQ // ATT_K + 1


def _dil_kernel(slopes_ref, qT_ref, k_ref, vT_ref, dist_ref, logc_ref, oT_ref,
                q2_ref, bias_ref, va_ref, *s_refs):
    per_tile = ATT_Q // ATT_K
    d = DIL_HEAD_DIM

    @pl.when(pl.program_id(1) == 0)
    def _():
        for g in range(HEADS_PER_STEP):
            slope = slopes_ref[pl.program_id(0) * HEADS_PER_STEP + g]
            bias_ref[g] = logc_ref[...] - slope * dist_ref[...]

    heads = []
    for g in range(HEADS_PER_STEP):
        rows = pl.ds(g * d, d)
        pair = pl.ds((g * d // LANES) * LANES, LANES)
        q2_ref[g] = jnp.zeros(q2_ref.shape[1:], BF16)
        q2_ref[g, pl.ds(g * d % LANES, d), :] = qT_ref[rows, :]
        _values_with_ones(vT_ref[rows, :], va_ref.at[g])

        def bias_chunk(i, j, g=g):
            return jnp.concatenate(
                [bias_ref[g, i * per_tile + c - j + 1] for c in range(per_tile)], axis=1)

        heads.append(_Head(q2_ref.at[g], k_ref.at[:, pair], va_ref.at[g],
                           oT_ref.at[rows, :], bias_chunk))
    _causal_attention(heads, s_refs, BF16)


def _dil_call(qdT, kd, vdT, batch):
    T = kd.shape[0]
    g = HEADS_PER_STEP
    width = g * DIL_HEAD_DIM
    assert width % LANES == 0
    slopes = jnp.asarray(
        [LOG2E * 2.0 ** (-8.0 * (i + 1) / DIL_HEADS) for i in range(DIL_HEADS)], F32)
    dist, logc = _dilated_tables()
    table = pl.BlockSpec(dist.shape, lambda h, b: (0, 0, 0), pipeline_mode=pl.Buffered(1))
    return pl.pallas_call(
        _dil_kernel,
        grid=(DIL_HEADS // g, batch),
        in_specs=[
            pl.BlockSpec(memory_space=pltpu.SMEM),
            pl.BlockSpec((width, SEQ), lambda h, b: (h, b)),
            pl.BlockSpec((SEQ, width), lambda h, b: (b, h)),
            pl.BlockSpec((width, SEQ), lambda h, b: (h, b)),
            table, table,
        ],
        out_specs=pl.BlockSpec((width, SEQ), lambda h, b: (h, b)),
        out_shape=jax.ShapeDtypeStruct((DIL_WIDTH, T), BF16),
        scratch_shapes=[pltpu.VMEM((g, LANES, SEQ), BF16),
                        pltpu.VMEM((g, N_DIL_OFFSETS, ATT_K, ATT_K), F32)]
        + _att_scratch(DIL_HEAD_DIM),
        compiler_params=pltpu.CompilerParams(
            dimension_semantics=("arbitrary", "arbitrary"), vmem_limit_bytes=VMEM_LIMIT),
        name="dil_attn",
    )(slopes, qdT, kd, vdT, dist, logc)


def _dilated_tables():
    off = np.arange(-1, N_DIL_OFFSETS - 1)[:, None, None]
    key = np.arange(ATT_K)[None, :, None]
    qry = np.arange(ATT_K)[None, None, :]
    dist = ATT_K * off + qry - key
    count = np.zeros(dist.shape, np.int32)
    for window, dilation in DIL_PATTERNS:
        count += ((dist >= 0) & (dist % dilation == 0) & (dist <= window)).astype(np.int32)
    logc = np.where(count > 0, np.log2(np.maximum(count, 1)), NEG)
    return jnp.asarray(dist, F32), jnp.asarray(logc, F32)


def _post_kernel(alpha, x_ref, oaT_ref, obT_ref, g_ref, woa_ref, wob_ref, wout_ref,
                 l1g_ref, l1b_ref, w1_ref, w2_ref, l2g_ref, l2b_ref, out_ref):
    n = x_ref.shape[0] // POST_ROW_GROUPS
    groups = [pl.ds(r * n, n) for r in range(POST_ROW_GROUPS)]
    ya = [_dot(oaT_ref[:, rows].T, woa_ref[...]) for rows in groups]
    yb = [_dot(obT_ref[:, rows].T, wob_ref[...]) for rows in groups]
    mixed = []
    for r, rows in enumerate(groups):
        g = g_ref[rows, :].astype(F32)
        mix = (g[:, :D_MODEL] * ya[r] + g[:, D_MODEL:] * yb[r]).astype(BF16)
        mixed.append(_dot(mix, wout_ref[...]))
    h = [_layer_norm(alpha * x_ref[rows, :] + mixed[r], l1g_ref[...], l1b_ref[...])
         for r, rows in enumerate(groups)]
    hb = [t.astype(BF16) for t in h]
    f = [None] * POST_ROW_GROUPS
    for c in range(D_FF // FF_CHUNK):
        cols = slice(c * FF_CHUNK, (c + 1) * FF_CHUNK)
        u = [jnp.maximum(_dot(t, w1_ref[:, cols]), 0.0) for t in hb]
        for r in range(POST_ROW_GROUPS):
            fc = _dot((u[r] * u[r]).astype(BF16), w2_ref[cols, :])
            f[r] = fc if f[r] is None else f[r] + fc
    for r, rows in enumerate(groups):
        out_ref[rows, :] = _layer_norm(alpha * h[r] + f[r], l2g_ref[...], l2b_ref[...])


def _post_call(alpha, x2, oaT, obT, g, woa, wob, wout, l1g, l1b, w1, w2, l2g, l2b):
    T = x2.shape[0]
    tm = ROW_TILE
    const = lambda i: (0, 0)
    row = lambda i: (i, 0)
    col = lambda i: (0, i)

    def wspec(a):
        return pl.BlockSpec(a.shape, const, pipeline_mode=pl.Buffered(1))

    return pl.pallas_call(
        functools.partial(_post_kernel, alpha),
        grid=(T // tm,),
        in_specs=[
            pl.BlockSpec((tm, D_MODEL), row),
            pl.BlockSpec((MLA_WIDTH, tm), col),
            pl.BlockSpec((DIL_WIDTH, tm), col),
            pl.BlockSpec((tm, N_BRANCH * D_MODEL), row),
            wspec(woa), wspec(wob), wspec(wout), wspec(l1g), wspec(l1b),
            wspec(w1), wspec(w2), wspec(l2g), wspec(l2b),
        ],
        out_specs=pl.BlockSpec((tm, D_MODEL), row),
        out_shape=jax.ShapeDtypeStruct((T, D_MODEL), F32),
        compiler_params=pltpu.CompilerParams(
            dimension_semantics=("arbitrary",), vmem_limit_bytes=VMEM_LIMIT),
        name="post",
    )(x2, oaT, obT, g, woa, wob, wout, l1g, l1b, w1, w2, l2g, l2b)


def _rotate_half_rows(w):
    half = w.shape[0] // 2
    return jnp.concatenate([-w[half:], w[:half]], axis=0)


def _rope_tables():
    inv = jnp.power(ROPE_THETA, -jnp.arange(ROPE_HALF, dtype=F32) / ROPE_HALF)
    ang = jnp.arange(SEQ).astype(F32)[:, None] * inv[None, :]
    cos, sin = jnp.cos(ang), jnp.sin(ang)
    qscale = (MLA_QK ** -0.5) * LOG2E
    cq = (cos * qscale).T
    sq = (sin * qscale).T
    zeros = jnp.zeros((SEQ, MLA_NOPE), F32)
    z32 = jnp.zeros((SEQ, MLA_ROPE), F32)
    ck = jnp.concatenate([zeros, cos, cos, z32], axis=1)
    sk = jnp.concatenate([zeros, z32, sin, sin], axis=1)
    return cq, sq, ck, sk


def _layer(x2, batch, alpha, w_in, b_gate, g_q_a, w_uq, g_kv_a, w_ukv, w_o_mla, w_o_dil,
           w_out, ln1_g, ln1_b, w_ff1, w_ff2, ln2_g, ln2_b):
    s0, s1, s2, s3 = Q_LORA, Q_LORA + KV_LORA, Q_LORA + KV_LORA + MLA_ROPE, \
        Q_LORA + KV_LORA + MLA_ROPE + 3 * DIL_WIDTH
    w_inT = w_in.T.astype(BF16)
    w_kr = w_inT[s1:s2]
    win = jnp.concatenate(
        [w_inT[:s1], jnp.zeros((MLA_NOPE, D_MODEL), BF16), w_kr, _rotate_half_rows(w_kr),
         w_inT[s2:]], axis=0)

    wq = w_uq.reshape(Q_LORA, MLA_HEADS, MLA_QK)
    wq = jnp.pad(wq, ((0, 0), (0, 0), (0, HEAD_PAD - MLA_QK)))
    wqT = wq.reshape(Q_LORA, MLA_HEADS * HEAD_PAD).T.astype(BF16)
    wkv = w_ukv.reshape(KV_LORA, MLA_HEADS, MLA_NOPE + MLA_V)
    wk = jnp.pad(wkv[:, :, :MLA_NOPE], ((0, 0), (0, 0), (0, HEAD_PAD - MLA_NOPE)))
    wk = wk.reshape(KV_LORA, MLA_HEADS * HEAD_PAD).astype(BF16)
    wvT = wkv[:, :, MLA_NOPE:].reshape(KV_LORA, MLA_WIDTH).T.astype(BF16)

    cq, sq, ck, sk = _rope_tables()
    qT, k, vT, qdT, kd, vdT, g = _proj_call(
        x2, win, wqT, wk, wvT, g_q_a[None], g_kv_a[None],
        b_gate.reshape(1, N_BRANCH * D_MODEL), cq, sq, ck, sk)

    oaT = _mla_call(qT, k, vT, batch)
    obT = _dil_call(qdT, kd, vdT, batch)

    return _post_call(alpha, x2, oaT, obT, g, w_o_mla.astype(BF16), w_o_dil.astype(BF16),
                      w_out.astype(BF16), ln1_g[None], ln1_b[None], w_ff1.astype(BF16),
                      w_ff2.astype(BF16), ln2_g[None], ln2_b[None])


def kernel(x, w_in, b_gate, g_q_a, w_uq, g_kv_a, w_ukv, w_o_mla, w_o_dil, w_out,
           ln1_g, ln1_b, w_ff1, w_ff2, ln2_g, ln2_b):
    batch, seq, d = x.shape
    assert (seq, d) == (SEQ, D_MODEL)
    depth = w_in.shape[0]
    alpha = (2 * depth) ** 0.25
    x2 = x.reshape(batch * seq, d)
    for l in range(depth):
        x2 = _layer(x2, batch, alpha, w_in[l], b_gate[l], g_q_a[l], w_uq[l], g_kv_a[l],
                    w_ukv[l], w_o_mla[l], w_o_dil[l], w_out[l], ln1_g[l], ln1_b[l],
                    w_ff1[l], w_ff2[l], ln2_g[l], ln2_b[l])
    return x2.reshape(batch, seq, d)
```
